```python
import jax, jax.numpy as jnp
from jax import lax
import numpy as np

D_MODEL = 1024
BATCH = 4
SEQ = 4096
DEPTH = 4
DEC_BATCH = 128
DEC_SEQ = 4
PAST_LEN = 8192
PAGE_SIZE = 128

PLE_DIM = 256
FFN_DIM = 2816
GLA_HEADS = 4
GLA_DK = 64
GLA_DV = 128
GLA_GATE_RANK = 16
GLA_TAU = 16.0
GLA_CHUNK = 64
MLA_HEADS = 4
Q_LORA = 256
KV_LORA = 128
QK_NOPE = 128
QK_ROPE = 64
QK_HEAD = QK_NOPE + QK_ROPE
V_HEAD = 128
ROPE_THETA = 10000.0
ATTN_BLOCK = 128
MIX_WIDTH = GLA_HEADS * GLA_DV + MLA_HEADS * V_HEAD
EPS = 1e-6
IN_SIZES = (GLA_HEADS * GLA_DK, GLA_HEADS * GLA_DK, GLA_HEADS * GLA_DV, GLA_HEADS * GLA_DV, GLA_GATE_RANK, Q_LORA, KV_LORA, QK_ROPE)
IN_DIM = sum(IN_SIZES)

kernel_name = 'hybrid_gla_mla_macaron_decoder_step'


def rms_norm(x, g):
    xf = x.astype(jnp.float32)
    y = xf * lax.rsqrt(jnp.mean(xf * xf, axis=-1, keepdims=True) + EPS)
    return (y * g.astype(jnp.float32)).astype(x.dtype)


def swiglu(x, w_gu, w_d):
    gate, up = jnp.split(x @ w_gu, 2, axis=-1)
    return (jax.nn.silu(gate) * up) @ w_d


def rope_angles(pos):
    inv = ROPE_THETA ** (-jnp.arange(0, QK_ROPE, 2, dtype=jnp.float32) / QK_ROPE)
    ang = pos.astype(jnp.float32)[:, None] * inv[None, :]
    return jnp.cos(ang), jnp.sin(ang)


def apply_rope(x, cos, sin):
    x1, x2 = jnp.split(x, 2, axis=-1)
    c = cos.astype(x.dtype)
    s = sin.astype(x.dtype)
    return jnp.concatenate([x1 * c - x2 * s, x1 * s + x2 * c], axis=-1)


def mla_keys(c_kv, k_rope_raw, cos, sin, w_uk, g_k):
    k_nope = (c_kv @ w_uk).reshape(c_kv.shape[:-1] + (MLA_HEADS, QK_NOPE))
    k_pe = jnp.broadcast_to(k_rope_raw[..., None, :], k_nope.shape[:-1] + (QK_ROPE,))
    k = rms_norm(jnp.concatenate([k_nope, k_pe], axis=-1), g_k)
    k_pe = apply_rope(k[..., QK_NOPE:], cos[..., :, None, :], sin[..., :, None, :])
    return jnp.concatenate([k[..., :QK_NOPE], k_pe], axis=-1)


def mla_queries(c_q, cos, sin, g_cq, w_uq, g_q):
    q = (rms_norm(c_q, g_cq) @ w_uq).reshape(c_q.shape[:-1] + (MLA_HEADS, QK_HEAD))
    q = rms_norm(q, g_q)
    q_pe = apply_rope(q[..., QK_NOPE:], cos[:, None, :], sin[:, None, :])
    return jnp.concatenate([q[..., :QK_NOPE], q_pe], axis=-1)


def mla_prompt_attention(q, c_kv, k_rope_raw, w_uk, w_uv, g_k):
    B, S = q.shape[0], q.shape[1]
    cos, sin = rope_angles(jnp.arange(S))
    k = mla_keys(c_kv, k_rope_raw, cos, sin, w_uk, g_k)
    scale = QK_HEAD ** -0.5
    kpos = jnp.arange(S)

    def block(j):
        q0 = j * ATTN_BLOCK
        qb = lax.dynamic_slice_in_dim(q, q0, ATTN_BLOCK, axis=1)
        s = jnp.einsum('bqhd,bkhd->bhqk', qb, k).astype(jnp.float32) * scale
        qpos = q0 + jnp.arange(ATTN_BLOCK)
        s = jnp.where(kpos[None, :] <= qpos[:, None], s, -jnp.inf)
        p = jax.nn.softmax(s, axis=-1).astype(c_kv.dtype)
        return jnp.einsum('bhqk,bkc->bqhc', p, c_kv)

    lat = lax.map(block, jnp.arange(S // ATTN_BLOCK))
    lat = jnp.moveaxis(lat, 0, 1).reshape(B, S, MLA_HEADS, KV_LORA)
    return jnp.einsum('bshc,chv->bshv', lat, w_uv.reshape(KV_LORA, MLA_HEADS, V_HEAD))


def mla_sample_attention(q, c_new, kr_new, cache_ckv, cache_krope, layer, page_table, w_uk, w_uv, g_k):
    T = q.shape[1]
    past = page_table.shape[1] * cache_ckv.shape[2]
    cos, sin = rope_angles(jnp.arange(past + T))
    kpos = jnp.arange(past + T)
    qpos = past + jnp.arange(T)
    mask = kpos[None, :] <= qpos[:, None]
    scale = QK_HEAD ** -0.5

    def one(args):
        pt, qs, cs, krs = args
        c_all = jnp.concatenate([cache_ckv[layer, pt].reshape(past, KV_LORA), cs], axis=0)
        kr_all = jnp.concatenate([cache_krope[layer, pt].reshape(past, QK_ROPE), krs], axis=0)
        k = mla_keys(c_all, kr_all, cos, sin, w_uk, g_k)
        s = jnp.einsum('qhd,khd->hqk', qs, k).astype(jnp.float32) * scale
        s = jnp.where(mask, s, -jnp.inf)
        p = jax.nn.softmax(s, axis=-1).astype(c_all.dtype)
        return jnp.einsum('hqk,kc->qhc', p, c_all)

    lat = lax.map(one, (page_table, q, c_new, kr_new))
    return jnp.einsum('bqhc,chv->bqhv', lat, w_uv.reshape(KV_LORA, MLA_HEADS, V_HEAD))


def gla_chunked(q, k, v, g, s0):
    B, T = q.shape[0], q.shape[1]
    C = min(GLA_CHUNK, T)
    N = -(-T // C)
    pad = N * C - T
    if pad:
        padw = ((0, 0), (0, pad), (0, 0), (0, 0))
        q, k, v, g = (jnp.pad(a, padw) for a in (q, k, v, g))
    q, k, v, g = (a.reshape((B, N, C) + a.shape[2:]) for a in (q, k, v, g))
    b = jnp.cumsum(g, axis=2)
    b_last = b[:, :, -1]
    q_t = q * jnp.exp(b)
    k_t = k * jnp.exp(-b)
    causal = jnp.tril(jnp.ones((C, C), dtype=bool))
    a = jnp.where(causal, jnp.einsum('bnchd,bnshd->bnhcs', q_t, k_t), 0.0)
    o_intra = jnp.einsum('bnhcs,bnshv->bnchv', a, v)
    ds = jnp.einsum('bnchd,bnchv->bnhdv', k * jnp.exp(b_last[:, :, None] - b), v)
    decay = jnp.exp(b_last)

    def step(s, inp):
        d, dsn = inp
        return d[..., None] * s + dsn, s

    s_fin, s_prev = lax.scan(step, s0, (jnp.moveaxis(decay, 1, 0), jnp.moveaxis(ds, 1, 0)))
    s_prev = jnp.moveaxis(s_prev, 0, 1)
    o_inter = jnp.einsum('bnchd,bnhdv->bnchv', q_t, s_prev)
    o = (o_intra + o_inter).reshape(B, N * C, GLA_HEADS, GLA_DV)[:, :T]
    return o, s_fin


def _layer(x, p_l, s0, pos0, attend, g_ffn1, w_ffn1_gu, w_ffn1_d, g_mix, w_in, w_gla_gate, b_gla_gate, g_gla_out, g_cq, w_uq, g_ckv, g_q, w_out, g_ffn2, w_ffn2_gu, w_ffn2_d, g_ple, w_ple_gate, w_ple):
    B, T = x.shape[0], x.shape[1]
    f32 = jnp.float32
    x = x + 0.5 * swiglu(rms_norm(x, g_ffn1), w_ffn1_gu, w_ffn1_d)
    u = rms_norm(x, g_mix)
    splits = [int(s) for s in np.cumsum(IN_SIZES)[:-1]]
    qa, ka, va, ra, ga, cq, ckv, kr = jnp.split(u @ w_in, splits, axis=-1)
    hd = lambda a, d: a.reshape(B, T, GLA_HEADS, d).astype(f32)
    log_a = jax.nn.log_sigmoid(hd(ga @ w_gla_gate + b_gla_gate, GLA_DK)) / GLA_TAU
    o_a, s_new = gla_chunked(hd(qa, GLA_DK) * (GLA_DK ** -0.5), hd(ka, GLA_DK), hd(va, GLA_DV), log_a, s0.astype(f32))
    o_a = rms_norm(o_a.astype(x.dtype), g_gla_out).reshape(B, T, GLA_HEADS * GLA_DV) * jax.nn.silu(ra)
    c_kv = rms_norm(ckv, g_ckv)
    cos, sin = rope_angles(pos0 + jnp.arange(T))
    q = mla_queries(cq, cos, sin, g_cq, w_uq, g_q)
    o_b = attend(q, c_kv, kr).reshape(B, T, MLA_HEADS * V_HEAD)
    x = x + jnp.concatenate([o_a, o_b], axis=-1) @ w_out
    x = x + 0.5 * swiglu(rms_norm(x, g_ffn2), w_ffn2_gu, w_ffn2_d)
    x = x + jax.nn.sigmoid(rms_norm(x, g_ple) @ w_ple_gate) * (p_l @ w_ple)
    return x, c_kv, kr, s_new.astype(s0.dtype)


def setup_inputs(seed: int = 0) -> dict:
    key = jax.random.key(seed)
    k = jax.random.split(key, 40)
    f32 = jnp.float32
    n_pages = PAST_LEN // PAGE_SIZE
    n_used = DEC_BATCH * n_pages
    n_pool = n_used + max(1, n_used // 4)
    nrm = lambda kk, shape, scale=1.0: jax.random.normal(kk, shape, f32) * scale
    gain = lambda kk, shape: 1.0 + 0.05 * jax.random.normal(kk, shape, f32)
    page_table = jax.random.permutation(k[5], n_pool)[:n_used].reshape(DEC_BATCH, n_pages).astype(jnp.int32)
    return {
        'x_prompt': nrm(k[0], (BATCH, SEQ, D_MODEL)),
        'x_sample': nrm(k[1], (DEC_BATCH, DEC_SEQ, D_MODEL)),
        'cache_ckv': nrm(k[2], (DEPTH, n_pool, PAGE_SIZE, KV_LORA)),
        'cache_krope': nrm(k[3], (DEPTH, n_pool, PAGE_SIZE, QK_ROPE)),
        'state_gla': nrm(k[4], (DEPTH, DEC_BATCH, GLA_HEADS, GLA_DK, GLA_DV)),
        'page_table': page_table,
        'p_prompt': nrm(k[6], (DEPTH, BATCH, SEQ, PLE_DIM)),
        'p_sample': nrm(k[7], (DEPTH, DEC_BATCH, DEC_SEQ, PLE_DIM)),
        'g_ffn1': gain(k[8], (DEPTH, D_MODEL)),
        'w_ffn1_gu': nrm(k[9], (DEPTH, D_MODEL, 2 * FFN_DIM), D_MODEL ** -0.5),
        'w_ffn1_d': nrm(k[10], (DEPTH, FFN_DIM, D_MODEL), FFN_DIM ** -0.5),
        'g_mix': gain(k[11], (DEPTH, D_MODEL)),
        'w_in': nrm(k[12], (DEPTH, D_MODEL, IN_DIM), D_MODEL ** -0.5),
        'w_gla_gate': nrm(k[13], (DEPTH, GLA_GATE_RANK, GLA_HEADS * GLA_DK), GLA_GATE_RANK ** -0.5),
        'b_gla_gate': nrm(k[14], (DEPTH, GLA_HEADS * GLA_DK), 0.1),
        'g_gla_out': gain(k[15], (DEPTH, GLA_DV)),
        'g_cq': gain(k[16], (DEPTH, Q_LORA)),
        'w_uq': nrm(k[17], (DEPTH, Q_LORA, MLA_HEADS * QK_HEAD), Q_LORA ** -0.5),
        'g_ckv': gain(k[18], (DEPTH, KV_LORA)),
        'w_uk': nrm(k[19], (DEPTH, KV_LORA, MLA_HEADS * QK_NOPE), KV_LORA ** -0.5),
        'w_uv': nrm(k[20], (DEPTH, KV_LORA, MLA_HEADS * V_HEAD), KV_LORA ** -0.5),
        'g_q': gain(k[21], (DEPTH, QK_HEAD)),
        'g_k': gain(k[22], (DEPTH, QK_HEAD)),
        'w_out': nrm(k[23], (DEPTH, MIX_WIDTH, D_MODEL), MIX_WIDTH ** -0.5),
        'g_ffn2': gain(k[24], (DEPTH, D_MODEL)),
        'w_ffn2_gu': nrm(k[25], (DEPTH, D_MODEL, 2 * FFN_DIM), D_MODEL ** -0.5),
        'w_ffn2_d': nrm(k[26], (DEPTH, FFN_DIM, D_MODEL), FFN_DIM ** -0.5),
        'g_ple': gain(k[27], (DEPTH, D_MODEL)),
        'w_ple_gate': nrm(k[28], (DEPTH, D_MODEL, D_MODEL), D_MODEL ** -0.5),
        'w_ple': nrm(k[29], (DEPTH, PLE_DIM, D_MODEL), PLE_DIM ** -0.5),
    }


def reference(x_prompt, x_sample, cache_ckv, cache_krope, state_gla, page_table, p_prompt, p_sample, g_ffn1, w_ffn1_gu, w_ffn1_d, g_mix, w_in, w_gla_gate, b_gla_gate, g_gla_out, g_cq, w_uq, g_ckv, w_uk, w_uv, g_q, g_k, w_out, g_ffn2, w_ffn2_gu, w_ffn2_d, g_ple, w_ple_gate, w_ple):
    past = page_table.shape[1] * cache_ckv.shape[2]
    hp, hs = x_prompt, x_sample
    s0_prompt = jnp.zeros((x_prompt.shape[0], GLA_HEADS, GLA_DK, GLA_DV), x_prompt.dtype)
    ckv_p, kr_p, st_p, ckv_s, kr_s, st_s = [], [], [], [], [], []
    for l in range(DEPTH):
        lw = (g_ffn1[l], w_ffn1_gu[l], w_ffn1_d[l], g_mix[l], w_in[l], w_gla_gate[l], b_gla_gate[l], g_gla_out[l], g_cq[l], w_uq[l], g_ckv[l], g_q[l], w_out[l], g_ffn2[l], w_ffn2_gu[l], w_ffn2_d[l], g_ple[l], w_ple_gate[l], w_ple[l])

        def attend_prompt(q, c, kr, l=l):
            return mla_prompt_attention(q, c, kr, w_uk[l], w_uv[l], g_k[l])

        def attend_sample(q, c, kr, l=l):
            return mla_sample_attention(q, c, kr, cache_ckv, cache_krope, l, page_table, w_uk[l], w_uv[l], g_k[l])

        hp, c, kr, st = _layer(hp, p_prompt[l], s0_prompt, 0, attend_prompt, *lw)
        ckv_p.append(c)
        kr_p.append(kr)
        st_p.append(st)
        hs, c, kr, st = _layer(hs, p_sample[l], state_gla[l], past, attend_sample, *lw)
        ckv_s.append(c)
        kr_s.append(kr)
        st_s.append(st)
    return (hp, hs, jnp.stack(ckv_p), jnp.stack(kr_p), jnp.stack(st_p), jnp.stack(ckv_s), jnp.stack(kr_s), jnp.stack(st_s))
```

```python
import functools

import jax
import jax.numpy as jnp
from jax import lax
from jax.experimental import pallas as pl
from jax.experimental.pallas import tpu as pltpu

F32 = jnp.float32
BF16 = jnp.bfloat16

EPS = 1e-6
GLA_HEADS = 4
GLA_DK = 64
GLA_DV = 128
GLA_GATE_RANK = 16
GLA_TAU = 16.0
GLA_CHUNK = 64
MLA_HEADS = 4
Q_LORA = 256
KV_LORA = 128
QK_NOPE = 128
QK_ROPE = 64
QK_HEAD = QK_NOPE + QK_ROPE
V_HEAD = 128
ROPE_THETA = 10000.0
QK_SCALE = QK_HEAD ** -0.5

LANES = 128
SUBLANES = 8
HEAD_PAD = 256
T_PAD = SUBLANES
VMEM_LIMIT = 56 * 1024 * 1024

ZC_Q, ZC_K, ZC_V, ZC_R, ZC_CQ, ZC_CKV, ZC_KRD, ZC_GA, ZC_END = 0, 256, 512, 1024, 1536, 1792, 1920, 2048, 2176
GC_Q, GC_K, GC_V, GC_R, GC_G, GC_END = 0, 256, 512, 1024, 1536, 1792

NT_DIMS = (((1,), (1,)), ((), ()))


def _const_spec(shape):
    nd = len(shape)
    return pl.BlockSpec(shape, lambda *_: (0,) * nd, pipeline_mode=pl.Buffered(1))


def _pick_tile(n, candidates):
    for c in candidates:
        if n % c == 0:
            return c
    raise ValueError(f"no tile in {candidates} divides {n}")


def _rms(x, g):
    return x * lax.rsqrt(jnp.mean(x * x, axis=-1, keepdims=True) + EPS) * g


def _silu(x):
    return x * jax.nn.sigmoid(x)


def _split3(x):
    hi = x.astype(BF16)
    r1 = x - hi.astype(F32)
    mid = r1.astype(BF16)
    lo = (r1 - mid.astype(F32)).astype(BF16)
    return hi, mid, lo


def _dot(a, b):
    return jnp.dot(a, b, preferred_element_type=F32)


def _dot_nt(a, b):
    return lax.dot_general(a, b, NT_DIMS, preferred_element_type=F32)


def _dot3(m, parts):
    return _dot(m, parts[0]) + _dot(m, parts[1]) + _dot(m, parts[2])


def _ffn_kernel(x_ref, g_ref, wgu_ref, wd_ref, o_ref, acc_ref, *, ffn, ch):
    x = x_ref[...]
    xn = _rms(x, g_ref[...]).astype(BF16)
    for c in range(ffn // ch):
        gate = _dot(xn, wgu_ref[:, c * ch:(c + 1) * ch])
        up = _dot(xn, wgu_ref[:, ffn + c * ch:ffn + (c + 1) * ch])
        hm = (_silu(gate) * up).astype(BF16)
        part = _dot(hm, wd_ref[c * ch:(c + 1) * ch, :])
        if c == 0:
            acc_ref[...] = part
        else:
            acc_ref[...] += part
    o_ref[...] = x + 0.5 * acc_ref[...]


def _ffn_call(x, g, wgu, wd):
    n, d = x.shape
    ffn = wd.shape[0]
    tm = _pick_tile(n, (768, 512, 256, 128))
    ch = _pick_tile(ffn, (256, 128))
    return pl.pallas_call(
        functools.partial(_ffn_kernel, ffn=ffn, ch=ch),
        grid=(n // tm,),
        in_specs=[
            pl.BlockSpec((tm, d), lambda i: (i, 0)),
            _const_spec((1, d)),
            _const_spec((d, 2 * ffn)),
            _const_spec((ffn, d)),
        ],
        out_specs=pl.BlockSpec((tm, d), lambda i: (i, 0)),
        out_shape=jax.ShapeDtypeStruct((n, d), F32),
        scratch_shapes=[pltpu.VMEM((tm, d), F32)],
        compiler_params=pltpu.CompilerParams(
            dimension_semantics=("arbitrary",), vmem_limit_bytes=VMEM_LIMIT),
        name="ffn",
    )(x, g, wgu, wd)


def _mixprep_kernel(h_ref, gmix_ref, win_ref, wgate_ref, bgate_ref, gcq_ref, wuq_ref,
                    gqn_ref, gqd_ref, gckv_ref, wuk_ref, gkn_ref, gkd_ref, t_ref,
                    gla_ref, ckvn_ref, kr_ref, qh_ref, kh_ref, vb_ref, qabs_ref):
    u = _rms(h_ref[...], gmix_ref[...]).astype(BF16)
    z = _dot(u, win_ref[...])

    gla_ref[:, GC_Q:GC_K] = z[:, ZC_Q:ZC_K] * (GLA_DK ** -0.5)
    gla_ref[:, GC_K:GC_G] = z[:, ZC_K:ZC_CQ]
    gx = _dot(z[:, ZC_GA:ZC_END].astype(BF16), wgate_ref[...]) + bgate_ref[...]
    log_sig = jnp.minimum(gx, 0.0) - jnp.log(1.0 + jnp.exp(-jnp.abs(gx)))
    gla_ref[:, GC_G:GC_END] = log_sig / GLA_TAU

    ckvn = _rms(z[:, ZC_CKV:ZC_KRD], gckv_ref[...])
    ckvn_ref[...] = ckvn
    cb = ckvn.astype(BF16)
    vb_ref[...] = cb
    krd = z[:, ZC_KRD:ZC_GA]
    kr_ref[...] = krd[:, 0:QK_ROPE]
    kr_ss = 0.5 * jnp.sum(krd * krd, axis=-1, keepdims=True)
    kn = _dot(cb, wuk_ref[...])

    cqn = _rms(z[:, ZC_CQ:ZC_CKV], gcq_ref[...]).astype(BF16)
    qq = _dot(cqn, wuq_ref[...])

    tab = t_ref[...]
    gqn, gqd, gkn, gkd = gqn_ref[...], gqd_ref[...], gkn_ref[...], gkd_ref[...]
    low = lax.broadcasted_iota(jnp.int32, (1, LANES), 1) < QK_ROPE
    nh = MLA_HEADS * QK_NOPE
    for hh in range(MLA_HEADS):
        lo, hi = hh * QK_NOPE, (hh + 1) * QK_NOPE
        qn = qq[:, lo:hi]
        qd = qq[:, nh + lo:nh + hi]
        ss = jnp.sum(qn * qn, axis=-1, keepdims=True) + 0.5 * jnp.sum(qd * qd, axis=-1, keepdims=True)
        rq = lax.rsqrt(ss / QK_HEAD + EPS) * QK_SCALE
        qnope = qn * rq * gqn
        aq = qd * rq * gqd * tab
        qrope = jnp.where(low, aq + pltpu.roll(aq, QK_ROPE, axis=1), 0.0)
        qh_ref[:, hh * HEAD_PAD:hh * HEAD_PAD + QK_NOPE] = qnope.astype(BF16)
        qh_ref[:, hh * HEAD_PAD + QK_NOPE:(hh + 1) * HEAD_PAD] = qrope.astype(BF16)
        qabs = _dot_nt((qnope * gkn).astype(BF16), wuk_ref[:, lo:hi])
        qabs_ref[:, lo:hi] = qabs.astype(BF16)

        knh = kn[:, lo:hi]
        rk = lax.rsqrt((jnp.sum(knh * knh, axis=-1, keepdims=True) + kr_ss) / QK_HEAD + EPS)
        kh_ref[:, hh * HEAD_PAD:hh * HEAD_PAD + QK_NOPE] = (knh * rk * gkn).astype(BF16)
        ak = krd * rk * gkd * tab
        krope = jnp.where(low, ak + pltpu.roll(ak, QK_ROPE, axis=1), 0.0)
        kh_ref[:, hh * HEAD_PAD + QK_NOPE:(hh + 1) * HEAD_PAD] = krope.astype(BF16)


def _mixprep_call(h, gmix, win, wgate, bgate, gcq, wuq, gqn, gqd, gckv, wuk, gkn, gkd, tab):
    n, d = h.shape
    tm = _pick_tile(n, (512, 256, 128))
    row = lambda w: pl.BlockSpec((tm, w), lambda i: (i, 0))
    consts = [gmix, win, wgate, bgate, gcq, wuq, gqn, gqd, gckv, wuk, gkn, gkd]
    hq = MLA_HEADS * HEAD_PAD
    out_shape = [
        jax.ShapeDtypeStruct((n, GC_END), F32),
        jax.ShapeDtypeStruct((n, KV_LORA), F32),
        jax.ShapeDtypeStruct((n, QK_ROPE), F32),
        jax.ShapeDtypeStruct((n, hq), BF16),
        jax.ShapeDtypeStruct((n, hq), BF16),
        jax.ShapeDtypeStruct((n, KV_LORA), BF16),
        jax.ShapeDtypeStruct((n, MLA_HEADS * KV_LORA), BF16),
    ]
    return pl.pallas_call(
        _mixprep_kernel,
        grid=(n // tm,),
        in_specs=[row(d)] + [_const_spec(c.shape) for c in consts] + [row(LANES)],
        out_specs=[row(s.shape[1]) for s in out_shape],
        out_shape=out_shape,
        compiler_params=pltpu.CompilerParams(
            dimension_semantics=("arbitrary",), vmem_limit_bytes=VMEM_LIMIT),
        name="mixprep",
    )(h, *consts, tab)


def _gla_out(o, ra, gout):
    outs = []
    for hh in range(GLA_HEADS):
        oh = o[:, hh * GLA_DV:(hh + 1) * GLA_DV]
        outs.append(_rms(oh, gout) * _silu(ra[:, hh * GLA_DV:(hh + 1) * GLA_DV]))
    return outs


def _gla_prompt_kernel(gin_ref, gout_ref, o_ref, st_ref, s_ref, *, tc):
    i = pl.program_id(1)
    c_len = GLA_CHUNK
    kd = GLA_HEADS * GLA_DK
    vd = GLA_HEADS * GLA_DV

    @pl.when(i == 0)
    def _():
        s_ref[...] = jnp.zeros_like(s_ref)

    ri = lax.broadcasted_iota(jnp.int32, (c_len, c_len), 0)
    ci = lax.broadcasted_iota(jnp.int32, (c_len, c_len), 1)
    tri = (ci <= ri).astype(BF16)
    k_head = lax.broadcasted_iota(jnp.int32, (1, kd), 1) // GLA_DK
    v_head = lax.broadcasted_iota(jnp.int32, (1, vd), 1) // GLA_DV
    wide_r = lax.broadcasted_iota(jnp.int32, (c_len, kd), 0)
    wide_s = lax.broadcasted_iota(jnp.int32, (c_len, kd), 1) % c_len
    causal_wide = wide_s <= wide_r
    bd = (lax.broadcasted_iota(jnp.int32, (kd, vd), 0) // GLA_DK
          == lax.broadcasted_iota(jnp.int32, (kd, vd), 1) // GLA_DV)
    gout = gout_ref[...]

    for c in range(tc // c_len):
        rows = slice(c * c_len, (c + 1) * c_len)
        q = gin_ref[rows, GC_Q:GC_K]
        k = gin_ref[rows, GC_K:GC_V]
        v = gin_ref[rows, GC_V:GC_R]
        ra = gin_ref[rows, GC_R:GC_G]
        g = gin_ref[rows, GC_G:GC_END]
        b = _dot3(tri, _split3(g))
        b_last = b[c_len - 1:c_len, :]
        q_t = (q * jnp.exp(b)).astype(BF16)
        k_t = k * jnp.exp(-b)
        k_dec = k * jnp.exp(b_last - b)
        vb = v.astype(BF16)

        kstack = jnp.concatenate(
            [jnp.where(k_head == hh, k_t, 0.0) for hh in range(GLA_HEADS)], axis=0).astype(BF16)
        a = jnp.where(causal_wide, _dot_nt(q_t, kstack), 0.0)
        vbd = jnp.concatenate(
            [jnp.where(v_head == hh, v, 0.0) for hh in range(GLA_HEADS)], axis=0).astype(BF16)
        s_prev = s_ref[...]
        o = _dot(a.astype(BF16), vbd) + _dot(q_t, s_prev.astype(BF16))

        kpad = jnp.concatenate(
            [k_dec, jnp.broadcast_to(b_last, (SUBLANES, kd)),
             jnp.zeros((LANES - c_len - SUBLANES, kd), F32)], axis=0)
        kdt = kpad.T
        decay_col = jnp.exp(kdt[:, c_len:c_len + 1])
        vpad = jnp.concatenate([vb, jnp.zeros((LANES - c_len, vd), BF16)], axis=0)
        ds = _dot(kdt.astype(BF16), vpad)
        s_ref[...] = decay_col * s_prev + jnp.where(bd, ds, 0.0)

        outs = _gla_out(o, ra, gout)
        for hh in range(GLA_HEADS):
            o_ref[rows, hh * GLA_DV:(hh + 1) * GLA_DV] = outs[hh].astype(BF16)

    @pl.when(i == pl.num_programs(1) - 1)
    def _():
        for hh in range(GLA_HEADS):
            st_ref[0, hh] = s_ref[hh * GLA_DK:(hh + 1) * GLA_DK, hh * GLA_DV:(hh + 1) * GLA_DV]


def _gla_prompt_call(gla, gout, batch, seq):
    tc = _pick_tile(seq, (256, 128, 64))
    nc = seq // tc
    vd = GLA_HEADS * GLA_DV
    return pl.pallas_call(
        functools.partial(_gla_prompt_kernel, tc=tc),
        grid=(batch, nc),
        in_specs=[pl.BlockSpec((tc, GC_END), lambda b, i: (b * nc + i, 0)), _const_spec((1, GLA_DV))],
        out_specs=[
            pl.BlockSpec((tc, vd), lambda b, i: (b * nc + i, 0)),
            pl.BlockSpec((1, GLA_HEADS, GLA_DK, GLA_DV), lambda b, i: (b, 0, 0, 0)),
        ],
        out_shape=[
            jax.ShapeDtypeStruct((batch * seq, vd), BF16),
            jax.ShapeDtypeStruct((batch, GLA_HEADS, GLA_DK, GLA_DV), F32),
        ],
        scratch_shapes=[pltpu.VMEM((GLA_HEADS * GLA_DK, vd), F32)],
        compiler_params=pltpu.CompilerParams(
            dimension_semantics=("arbitrary", "arbitrary"), vmem_limit_bytes=VMEM_LIMIT),
        name="gla_prompt",
    )(gla, gout)


def _gla_sample_kernel(gin_ref, s0_ref, gout_ref, o_ref, sn_ref, *, nreq):
    rows = nreq * T_PAD
    kd = GLA_HEADS * GLA_DK
    q = gin_ref[:, GC_Q:GC_K]
    k = gin_ref[:, GC_K:GC_V]
    v = gin_ref[:, GC_V:GC_R]
    ra = gin_ref[:, GC_R:GC_G]
    g = gin_ref[:, GC_G:GC_END]
    ri = lax.broadcasted_iota(jnp.int32, (rows, rows), 0)
    ci = lax.broadcasted_iota(jnp.int32, (rows, rows), 1)
    same = (ri // T_PAD) == (ci // T_PAD)
    causal = same & (ci <= ri)
    g3 = _split3(g)
    b = _dot3(causal.astype(BF16), g3)
    b_last = _dot3(same.astype(BF16), g3)
    q_t = q * jnp.exp(b)
    k_t = (k * jnp.exp(-b)).astype(BF16)
    k_dec = k * jnp.exp(b_last - b)
    vb = v.astype(BF16)
    k_head = lax.broadcasted_iota(jnp.int32, (1, kd), 1) // GLA_DK

    o_intra = []
    for hh in range(GLA_HEADS):
        a = _dot_nt(jnp.where(k_head == hh, q_t, 0.0).astype(BF16), k_t)
        a = jnp.where(causal, a, 0.0).astype(BF16)
        o_intra.append(_dot(a, vb[:, hh * GLA_DV:(hh + 1) * GLA_DV]))

    o_inter = []
    for r in range(nreq):
        qr = q_t[r * T_PAD:(r + 1) * T_PAD, :]
        qm = jnp.concatenate(
            [jnp.where(k_head == hh, qr, 0.0) for hh in range(GLA_HEADS)], axis=0).astype(BF16)
        o_inter.append(_dot(qm, s0_ref[r * kd:(r + 1) * kd, :].astype(BF16)))

    gout = gout_ref[...]
    for hh in range(GLA_HEADS):
        inter = jnp.concatenate(
            [o_inter[r][hh * T_PAD:(hh + 1) * T_PAD, :] for r in range(nreq)], axis=0)
        oh = o_intra[hh] + inter
        y = _rms(oh, gout) * _silu(ra[:, hh * GLA_DV:(hh + 1) * GLA_DV])
        o_ref[:, hh * GLA_DV:(hh + 1) * GLA_DV] = y.astype(BF16)

    kdt = k_dec.T
    blt = b_last.T
    lane = lax.broadcasted_iota(jnp.int32, (1, rows), 1)
    for hh in range(GLA_HEADS):
        kh = kdt[hh * GLA_DK:(hh + 1) * GLA_DK, :]
        bh = blt[hh * GLA_DK:(hh + 1) * GLA_DK, :]
        lhs = jnp.concatenate(
            [jnp.where(lane // T_PAD == r, kh, 0.0) for r in range(nreq)], axis=0).astype(BF16)
        ds = _dot(lhs, vb[:, hh * GLA_DV:(hh + 1) * GLA_DV])
        for r in range(nreq):
            dec = jnp.exp(jnp.sum(jnp.where(lane == r * T_PAD, bh, 0.0), axis=-1, keepdims=True))
            lo = r * kd + hh * GLA_DK
            sn_ref[lo:lo + GLA_DK, :] = dec * s0_ref[lo:lo + GLA_DK, :] + ds[r * GLA_DK:(r + 1) * GLA_DK, :]


def _gla_sample_call(gla_pad, s0, gout, dec_batch):
    nreq = _pick_tile(dec_batch, (16, 8, 4, 2, 1))
    rows = nreq * T_PAD
    kd = GLA_HEADS * GLA_DK
    vd = GLA_HEADS * GLA_DV
    return pl.pallas_call(
        functools.partial(_gla_sample_kernel, nreq=nreq),
        grid=(dec_batch // nreq,),
        in_specs=[
            pl.BlockSpec((rows, GC_END), lambda i: (i, 0)),
            pl.BlockSpec((nreq * kd, GLA_DV), lambda i: (i, 0)),
            _const_spec((1, GLA_DV)),
        ],
        out_specs=[
            pl.BlockSpec((rows, vd), lambda i: (i, 0)),
            pl.BlockSpec((nreq * kd, GLA_DV), lambda i: (i, 0)),
        ],
        out_shape=[
            jax.ShapeDtypeStruct((dec_batch * T_PAD, vd), BF16),
            jax.ShapeDtypeStruct((dec_batch * kd, GLA_DV), F32),
        ],
        compiler_params=pltpu.CompilerParams(
            dimension_semantics=("arbitrary",), vmem_limit_bytes=VMEM_LIMIT),
        name="gla_sample",
    )(gla_pad, s0, gout)


def _mla_prompt_kernel(q_ref, k_ref, v_ref, wuv_ref, o_ref, m_ref, l_ref, acc_ref, *, tq):
    i = pl.program_id(1)
    m_ref[...] = jnp.full(m_ref.shape, -jnp.inf, F32)
    l_ref[...] = jnp.zeros_like(l_ref)
    acc_ref[...] = jnp.zeros_like(acc_ref)
    ri = lax.broadcasted_iota(jnp.int32, (tq, tq), 0)
    ci = lax.broadcasted_iota(jnp.int32, (tq, tq), 1)
    diag_ok = ci <= ri

    def block(j, masked):
        k0 = pl.multiple_of(j * tq, tq)
        kblk = k_ref[pl.ds(k0, tq), :]
        vblk = v_ref[pl.ds(k0, tq), :]
        ps, alphas = [], []
        for hh in range(MLA_HEADS):
            s = _dot_nt(q_ref[:, hh * HEAD_PAD:(hh + 1) * HEAD_PAD], kblk[:, hh * HEAD_PAD:(hh + 1) * HEAD_PAD])
            if masked:
                s = jnp.where(diag_ok, s, -jnp.inf)
            m_prev = m_ref[hh]
            m_new = jnp.maximum(m_prev, jnp.max(s, axis=-1, keepdims=True))
            alpha = jnp.exp(m_prev - m_new)
            p = jnp.exp(s - m_new)
            l_ref[hh] = alpha * l_ref[hh] + jnp.sum(p, axis=-1, keepdims=True)
            m_ref[hh] = m_new
            ps.append(p.astype(BF16))
            alphas.append(alpha)
        pv = _dot(jnp.concatenate(ps, axis=0), vblk)
        for hh in range(MLA_HEADS):
            acc_ref[hh] = alphas[hh] * acc_ref[hh] + pv[hh * tq:(hh + 1) * tq, :]

    def body(j, carry):
        block(j, False)
        return carry

    lax.fori_loop(0, i, body, 0)
    block(i, True)
    for hh in range(MLA_HEADS):
        lat = (acc_ref[hh] / l_ref[hh]).astype(BF16)
        o_ref[:, hh * V_HEAD:(hh + 1) * V_HEAD] = _dot(lat, wuv_ref[:, hh * V_HEAD:(hh + 1) * V_HEAD]).astype(BF16)


def _mla_prompt_call(qh, kh, vb, wuv, batch, seq):
    tq = _pick_tile(seq, (512, 256, 128))
    nq = seq // tq
    hq = MLA_HEADS * HEAD_PAD
    od = MLA_HEADS * V_HEAD
    return pl.pallas_call(
        functools.partial(_mla_prompt_kernel, tq=tq),
        grid=(batch, nq),
        in_specs=[
            pl.BlockSpec((tq, hq), lambda b, i: (b * nq + i, 0)),
            pl.BlockSpec((seq, hq), lambda b, i: (b, 0)),
            pl.BlockSpec((seq, KV_LORA), lambda b, i: (b, 0)),
            _const_spec((KV_LORA, od)),
        ],
        out_specs=pl.BlockSpec((tq, od), lambda b, i: (b * nq + i, 0)),
        out_shape=jax.ShapeDtypeStruct((batch * seq, od), BF16),
        scratch_shapes=[
            pltpu.VMEM((MLA_HEADS, tq, 1), F32),
            pltpu.VMEM((MLA_HEADS, tq, 1), F32),
            pltpu.VMEM((MLA_HEADS, tq, KV_LORA), F32),
        ],
        compiler_params=pltpu.CompilerParams(
            dimension_semantics=("arbitrary", "arbitrary"), vmem_limit_bytes=VMEM_LIMIT),
        name="mla_prompt",
    )(qh, kh, vb, wuv)


def _mla_sample_kernel(pt_ref, qabs_ref, qpe_ref, cnew_ref, krnew_ref, wukt_ref, gpe_ref, cos_ref, sin_ref,
                       wuv_ref, ckv_hbm, kr_hbm, o_ref, cbuf, kbuf, cbf, sc_ref, sem,
                       *, layer, n_pages, page, kc):
    r = pl.program_id(0)
    nreq = pl.num_programs(0)
    slot = r % 2
    past = n_pages * page
    tail = LANES

    def page_copies(req, sl, p):
        pg = pt_ref[req * n_pages + p]
        dst = pl.ds(pl.multiple_of(p * page, page), page)
        return (pltpu.make_async_copy(ckv_hbm.at[layer, pg], cbuf.at[sl, dst], sem.at[sl, 0]),
                pltpu.make_async_copy(kr_hbm.at[layer, pg], kbuf.at[sl, dst], sem.at[sl, 1]))

    def issue(req, sl):
        def body(p, carry):
            for cp in page_copies(req, sl, p):
                cp.start()
            return carry
        lax.fori_loop(0, n_pages, body, 0)

    @pl.when(r == 0)
    def _():
        for sl in range(2):
            cbuf[sl, past:past + tail, :] = jnp.zeros((tail, KV_LORA), F32)
            kbuf[sl, past:past + tail, :] = jnp.zeros((tail, QK_ROPE), F32)
        issue(0, 0)

    @pl.when(r + 1 < nreq)
    def _():
        issue(r + 1, 1 - slot)

    def wait_body(p, carry):
        for cp in page_copies(r, slot, p):
            cp.wait()
        return carry
    lax.fori_loop(0, n_pages, wait_body, 0)

    cbuf[slot, past:past + T_PAD, :] = cnew_ref[0]
    kbuf[slot, past:past + T_PAD, :] = krnew_ref[0]

    nrow = MLA_HEADS * T_PAD
    lhs = jnp.concatenate([wukt_ref[...], qabs_ref[0]], axis=0)
    eye = (lax.broadcasted_iota(jnp.int32, (QK_ROPE, QK_ROPE), 0)
           == lax.broadcasted_iota(jnp.int32, (QK_ROPE, QK_ROPE), 1)).astype(BF16)
    gpe = gpe_ref[...]
    qpe = qpe_ref[0]
    half = QK_ROPE // 2
    nk = MLA_HEADS * QK_NOPE

    def chunk(k0, width, is_tail):
        cb = cbuf[slot, k0:k0 + width, :].astype(BF16)
        cbf[k0:k0 + width, :] = cb
        res = _dot_nt(lhs, cb)
        krt = _dot_nt(eye, kbuf[slot, k0:k0 + width, :].astype(BF16))
        ss_kr = jnp.sum(krt * krt, axis=0, keepdims=True)
        y = krt * gpe
        x1, x2 = y[0:half], y[half:QK_ROPE]
        cs, sn = cos_ref[:, k0:k0 + width], sin_ref[:, k0:k0 + width]
        rot = jnp.concatenate([x1 * cs - x2 * sn, x1 * sn + x2 * cs], axis=0).astype(BF16)
        s = res[nk:nk + nrow] + _dot(qpe, rot)
        rks = []
        for hh in range(MLA_HEADS):
            knh = res[hh * QK_NOPE:(hh + 1) * QK_NOPE]
            rk = lax.rsqrt((jnp.sum(knh * knh, axis=0, keepdims=True) + ss_kr) / QK_HEAD + EPS)
            rks.append(jnp.broadcast_to(rk, (T_PAD, width)))
        s = s * jnp.concatenate(rks, axis=0)
        if is_tail:
            tok = lax.broadcasted_iota(jnp.int32, (nrow, width), 0) % T_PAD
            key = lax.broadcasted_iota(jnp.int32, (nrow, width), 1)
            s = jnp.where(key <= tok, s, -jnp.inf)
        sc_ref[:, k0:k0 + width] = s

    for c in range(past // kc):
        chunk(c * kc, kc, False)
    chunk(past, tail, True)

    s_all = sc_ref[...]
    m = jnp.max(s_all, axis=-1, keepdims=True)
    p = jnp.exp(s_all - m)
    l = jnp.sum(p, axis=-1, keepdims=True)
    lat = _dot(p.astype(BF16), cbf[...]) / l
    for hh in range(MLA_HEADS):
        lh = lat[hh * T_PAD:(hh + 1) * T_PAD, :].astype(BF16)
        o_ref[0, hh * T_PAD:(hh + 1) * T_PAD, :] = _dot(lh, wuv_ref[:, hh * V_HEAD:(hh + 1) * V_HEAD]).astype(BF16)


def _mla_sample_call(page_table, qabs, qpe, cnew, krnew, wukt, gpe, cos_t, sin_t, wuv, cache_ckv, cache_krope, layer):
    dec_batch, n_pages = page_table.shape
    page = cache_ckv.shape[2]
    past = n_pages * page
    kc = _pick_tile(past, (1024, 512, 256, 128))
    nrow = MLA_HEADS * T_PAD
    total = past + LANES
    req = lambda w: pl.BlockSpec((1, nrow, w), lambda r, pt: (r, 0, 0))
    new = lambda w: pl.BlockSpec((1, T_PAD, w), lambda r, pt: (r, 0, 0))
    const = lambda shape: pl.BlockSpec(shape, lambda r, pt: (0,) * len(shape), pipeline_mode=pl.Buffered(1))
    grid_spec = pltpu.PrefetchScalarGridSpec(
        num_scalar_prefetch=1,
        grid=(dec_batch,),
        in_specs=[
            req(KV_LORA), req(QK_ROPE), new(KV_LORA), new(QK_ROPE),
            const(wukt.shape), const(gpe.shape), const(cos_t.shape), const(sin_t.shape), const(wuv.shape),
            pl.BlockSpec(memory_space=pl.ANY), pl.BlockSpec(memory_space=pl.ANY),
        ],
        out_specs=pl.BlockSpec((1, nrow, V_HEAD), lambda r, pt: (r, 0, 0)),
        scratch_shapes=[
            pltpu.VMEM((2, total, KV_LORA), F32),
            pltpu.VMEM((2, total, QK_ROPE), F32),
            pltpu.VMEM((total, KV_LORA), BF16),
            pltpu.VMEM((nrow, total), F32),
            pltpu.SemaphoreType.DMA((2, 2)),
        ],
    )
    return pl.pallas_call(
        functools.partial(_mla_sample_kernel, layer=layer, n_pages=n_pages, page=page, kc=kc),
        grid_spec=grid_spec,
        out_shape=jax.ShapeDtypeStruct((dec_batch, nrow, V_HEAD), BF16),
        compiler_params=pltpu.CompilerParams(
            dimension_semantics=("arbitrary",), vmem_limit_bytes=VMEM_LIMIT),
        name="mla_sample",
    )(page_table.reshape(-1), qabs, qpe, cnew, krnew, wukt, gpe, cos_t, sin_t, wuv, cache_ckv, cache_krope)


def _post_kernel(h_ref, oa_ref, ob_ref, wa_ref, wb_ref, o_ref):
    o_ref[...] = h_ref[...] + _dot(oa_ref[...], wa_ref[...]) + _dot(ob_ref[...], wb_ref[...])


def _post_call(h, oa, ob, wa, wb):
    n, d = h.shape
    tm = _pick_tile(n, (768, 512, 256, 128))
    row = lambda w: pl.BlockSpec((tm, w), lambda i: (i, 0))
    return pl.pallas_call(
        _post_kernel,
        grid=(n // tm,),
        in_specs=[row(d), row(oa.shape[1]), row(ob.shape[1]), _const_spec(wa.shape), _const_spec(wb.shape)],
        out_specs=row(d),
        out_shape=jax.ShapeDtypeStruct((n, d), F32),
        compiler_params=pltpu.CompilerParams(
            dimension_semantics=("arbitrary",), vmem_limit_bytes=VMEM_LIMIT),
        name="post",
    )(h, oa, ob, wa, wb)


def _ple_kernel(h_ref, p_ref, g_ref, wg_ref, wp_ref, o_ref):
    h = h_ref[...]
    gate = jax.nn.sigmoid(_dot(_rms(h, g_ref[...]).astype(BF16), wg_ref[...]))
    o_ref[...] = h + gate * _dot(p_ref[...].astype(BF16), wp_ref[...])


def _ple_call(h, p, g, wg, wp):
    n, d = h.shape
    tm = _pick_tile(n, (768, 512, 256, 128))
    row = lambda w: pl.BlockSpec((tm, w), lambda i: (i, 0))
    return pl.pallas_call(
        _ple_kernel,
        grid=(n // tm,),
        in_specs=[row(d), row(p.shape[1]), _const_spec(g.shape), _const_spec(wg.shape), _const_spec(wp.shape)],
        out_specs=row(d),
        out_shape=jax.ShapeDtypeStruct((n, d), F32),
        compiler_params=pltpu.CompilerParams(
            dimension_semantics=("arbitrary",), vmem_limit_bytes=VMEM_LIMIT),
        name="ple",
    )(h, p, g, wg, wp)


def _rope_tables(pos):
    inv = ROPE_THETA ** (-jnp.arange(0, QK_ROPE, 2, dtype=F32) / QK_ROPE)
    ang = pos.astype(F32)[:, None] * inv[None, :]
    return jnp.cos(ang), jnp.sin(ang)


def _dup_rope(x):
    x1, x2 = x[..., :QK_ROPE // 2], x[..., QK_ROPE // 2:]
    return jnp.concatenate([x1, x2, x2, x1], axis=-1)


def kernel(x_prompt, x_sample, cache_ckv, cache_krope, state_gla, page_table, p_prompt, p_sample, g_ffn1, w_ffn1_gu, w_ffn1_d, g_mix, w_in, w_gla_gate, b_gla_gate, g_gla_out, g_cq, w_uq, g_ckv, w_uk, w_uv, g_q, g_k, w_out, g_ffn2, w_ffn2_gu, w_ffn2_d, g_ple, w_ple_gate, w_ple):
    batch, seq, d = x_prompt.shape
    dec_batch, dec_seq, _ = x_sample.shape
    depth = w_in.shape[0]
    n_pages, page = page_table.shape[1], cache_ckv.shape[2]
    past = n_pages * page
    np_rows = batch * seq
    ns_rows = dec_batch * dec_seq
    assert dec_seq <= T_PAD

    kd = GLA_HEADS * GLA_DK
    vd = GLA_HEADS * GLA_DV
    o = 0
    cols = {}
    for name, w in (("q", kd), ("k", kd), ("v", vd), ("r", vd), ("ga", GLA_GATE_RANK), ("cq", Q_LORA),
                    ("ckv", KV_LORA), ("kr", QK_ROPE)):
        cols[name] = w_in[:, :, o:o + w]
        o += w
    win_p = jnp.concatenate(
        [cols["q"], cols["k"], cols["v"], cols["r"], cols["cq"], cols["ckv"], _dup_rope(cols["kr"]), cols["ga"],
         jnp.zeros((depth, d, LANES - GLA_GATE_RANK), F32)], axis=-1).astype(BF16)
    wgate_p = jnp.concatenate(
        [w_gla_gate, jnp.zeros((depth, LANES - GLA_GATE_RANK, kd), F32)], axis=1).astype(BF16)
    wuq4 = w_uq.reshape(depth, Q_LORA, MLA_HEADS, QK_HEAD)
    wuq_p = jnp.concatenate(
        [wuq4[..., :QK_NOPE].reshape(depth, Q_LORA, -1), _dup_rope(wuq4[..., QK_NOPE:]).reshape(depth, Q_LORA, -1)],
        axis=-1).astype(BF16)
    wuk_b = w_uk.astype(BF16)
    wukt_b = jnp.swapaxes(w_uk, 1, 2).astype(BF16)
    wuv_b = w_uv.astype(BF16)
    wout_b = w_out.astype(BF16)
    wf1gu, wf1d = w_ffn1_gu.astype(BF16), w_ffn1_d.astype(BF16)
    wf2gu, wf2d = w_ffn2_gu.astype(BF16), w_ffn2_d.astype(BF16)
    wpg_b, wp_b = w_ple_gate.astype(BF16), w_ple.astype(BF16)
    row2 = lambda a: a.reshape(depth, 1, -1)
    gqn, gqd = row2(g_q[:, :QK_NOPE]), row2(_dup_rope(g_q[:, QK_NOPE:]))
    gkn, gkd = row2(g_k[:, :QK_NOPE]), row2(_dup_rope(g_k[:, QK_NOPE:]))
    gpe_col = g_k[:, QK_NOPE:].reshape(depth, QK_ROPE, 1)

    pos = jnp.concatenate([jnp.tile(jnp.arange(seq), batch), jnp.tile(past + jnp.arange(dec_seq), dec_batch)])
    cs, sn = _rope_tables(pos)
    tab = jnp.concatenate([cs, cs, -sn, sn], axis=-1)
    cs_k, sn_k = _rope_tables(jnp.arange(past + LANES))
    cos_t, sin_t = cs_k.T, sn_k.T

    x = jnp.concatenate([x_prompt.reshape(np_rows, d), x_sample.reshape(ns_rows, d)], axis=0)
    p_all = jnp.concatenate(
        [p_prompt.reshape(depth, np_rows, -1), p_sample.reshape(depth, ns_rows, -1)], axis=1)
    s0_all = state_gla.reshape(depth, dec_batch * kd, GLA_DV)

    pad_t = lambda a: jnp.pad(a.reshape(dec_batch, dec_seq, -1), ((0, 0), (0, T_PAD - dec_seq), (0, 0)))
    unpad = lambda a: a.reshape(dec_batch, T_PAD, -1)[:, :dec_seq].reshape(ns_rows, -1)

    def heads_rows(a, width):
        a = a.reshape(dec_batch, dec_seq, MLA_HEADS, width).transpose(0, 2, 1, 3)
        a = jnp.pad(a, ((0, 0), (0, 0), (0, T_PAD - dec_seq), (0, 0)))
        return a.reshape(dec_batch, MLA_HEADS * T_PAD, width)

    outs = [[] for _ in range(6)]
    for l in range(depth):
        h1 = _ffn_call(x, row2(g_ffn1)[l], wf1gu[l], wf1d[l])
        gla, ckvn, kr, qh, kh, vb, qabs = _mixprep_call(
            h1, row2(g_mix)[l], win_p[l], wgate_p[l], row2(b_gla_gate)[l], row2(g_cq)[l], wuq_p[l], gqn[l], gqd[l],
            row2(g_ckv)[l], wuk_b[l], gkn[l], gkd[l], tab)
        gout = row2(g_gla_out)[l]

        oa_p, st_p = _gla_prompt_call(gla, gout, batch, seq)
        oa_s, st_s = _gla_sample_call(pad_t(gla[np_rows:]).reshape(dec_batch * T_PAD, -1), s0_all[l], gout, dec_batch)
        ob_p = _mla_prompt_call(qh, kh, vb, wuv_b[l], batch, seq)
        qh_s = qh[np_rows:].reshape(ns_rows, MLA_HEADS, HEAD_PAD)[:, :, QK_NOPE:QK_NOPE + QK_ROPE]
        ob_s = _mla_sample_call(
            page_table, heads_rows(qabs[np_rows:], KV_LORA), heads_rows(qh_s, QK_ROPE), pad_t(ckvn[np_rows:]),
            pad_t(kr[np_rows:]), wukt_b[l], gpe_col[l], cos_t, sin_t, wuv_b[l], cache_ckv, cache_krope, l)
        ob_s = ob_s.reshape(dec_batch, MLA_HEADS, T_PAD, V_HEAD)[:, :, :dec_seq].transpose(0, 2, 1, 3)

        oa = jnp.concatenate([oa_p, unpad(oa_s)], axis=0)
        ob = jnp.concatenate([ob_p, ob_s.reshape(ns_rows, -1)], axis=0)
        h2 = _post_call(h1, oa, ob, wout_b[l, :vd], wout_b[l, vd:])
        h3 = _ffn_call(h2, row2(g_ffn2)[l], wf2gu[l], wf2d[l])
        x = _ple_call(h3, p_all[l], row2(g_ple)[l], wpg_b[l], wp_b[l])

        outs[0].append(ckvn[:np_rows].reshape(batch, seq, KV_LORA))
        outs[1].append(kr[:np_rows].reshape(batch, seq, QK_ROPE))
        outs[2].append(st_p)
        outs[3].append(ckvn[np_rows:].reshape(dec_batch, dec_seq, KV_LORA))
        outs[4].append(kr[np_rows:].reshape(dec_batch, dec_seq, QK_ROPE))
        outs[5].append(st_s.reshape(dec_batch, GLA_HEADS, GLA_DK, GLA_DV))

    return (x[:np_rows].reshape(batch, seq, d), x[np_rows:].reshape(dec_batch, dec_seq, d),
            *(jnp.stack(o) for o in outs))
```

```python
import functools
import math

import jax
import jax.numpy as jnp
from jax import lax
from jax.experimental import pallas as pl
from jax.experimental.pallas import tpu as pltpu

F32 = jnp.float32
BF16 = jnp.bfloat16

EPS = 1e-6
GLA_HEADS = 4
GLA_DK = 64
GLA_DV = 128
GLA_GATE_RANK = 16
GLA_TAU = 16.0
GLA_CHUNK = 64
MLA_HEADS = 4
Q_LORA = 256
KV_LORA = 128
QK_NOPE = 128
QK_ROPE = 64
QK_HEAD = QK_NOPE + QK_ROPE
V_HEAD = 128
ROPE_THETA = 10000.0
QK_SCALE = QK_HEAD ** -0.5
LOG2E = 1.4426950408889634

LANES = 128
SUBLANES = 8
HEAD_PAD = 256
T_PAD = SUBLANES
VMEM_LIMIT = 56 * 1024 * 1024

ZC_Q, ZC_K, ZC_V, ZC_R, ZC_CQ, ZC_CKV, ZC_KRD, ZC_GA, ZC_END = 0, 256, 512, 1024, 1536, 1792, 1920, 2048, 2176
GC_Q, GC_K, GC_V, GC_R, GC_G, GC_END = 0, 256, 512, 1024, 1536, 1792

NT_DIMS = (((1,), (1,)), ((), ()))


def _const_spec(shape):
    nd = len(shape)
    return pl.BlockSpec(shape, lambda *_: (0,) * nd, pipeline_mode=pl.Buffered(1))


def _pick_tile(n, candidates):
    for c in candidates:
        if n % c == 0:
            return c
    raise ValueError(f"no tile in {candidates} divides {n}")


def _rms(x, g):
    return x * lax.rsqrt(jnp.mean(x * x, axis=-1, keepdims=True) + EPS) * g


def _silu(x):
    return x * jax.nn.sigmoid(x)


def _split3(x):
    hi = x.astype(BF16)
    r1 = x - hi.astype(F32)
    mid = r1.astype(BF16)
    lo = (r1 - mid.astype(F32)).astype(BF16)
    return hi, mid, lo


def _dot(a, b):
    return jnp.dot(a, b, preferred_element_type=F32)


def _dot_nt(a, b):
    return lax.dot_general(a, b, NT_DIMS, preferred_element_type=F32)


def _dot3(m, parts):
    return _dot(m, parts[0]) + _dot(m, parts[1]) + _dot(m, parts[2])


def _ffn_kernel(x_ref, g_ref, wgu_ref, wd_ref, o_ref, acc_ref, *, ffn, ch):
    x = x_ref[...]
    xn = _rms(x, g_ref[...]).astype(BF16)
    for c in range(ffn // ch):
        gate = _dot(xn, wgu_ref[:, c * ch:(c + 1) * ch])
        up = _dot(xn, wgu_ref[:, ffn + c * ch:ffn + (c + 1) * ch])
        hm = (_silu(gate) * up).astype(BF16)
        part = _dot(hm, wd_ref[c * ch:(c + 1) * ch, :])
        if c == 0:
            acc_ref[...] = part
        else:
            acc_ref[...] += part
    o_ref[...] = x + 0.5 * acc_ref[...]


def _ffn_call(x, g, wgu, wd):
    n, d = x.shape
    ffn = wd.shape[0]
    tm = _pick_tile(n, (768, 512, 256, 128))
    ch = _pick_tile(ffn, (256, 128))
    return pl.pallas_call(
        functools.partial(_ffn_kernel, ffn=ffn, ch=ch),
        grid=(n // tm,),
        in_specs=[
            pl.BlockSpec((tm, d), lambda i: (i, 0)),
            _const_spec((1, d)),
            _const_spec((d, 2 * ffn)),
            _const_spec((ffn, d)),
        ],
        out_specs=pl.BlockSpec((tm, d), lambda i: (i, 0)),
        out_shape=jax.ShapeDtypeStruct((n, d), F32),
        scratch_shapes=[pltpu.VMEM((tm, d), F32)],
        compiler_params=pltpu.CompilerParams(
            dimension_semantics=("arbitrary",), vmem_limit_bytes=VMEM_LIMIT),
        name="ffn",
    )(x, g, wgu, wd)


def _mixprep_kernel(h_ref, gmix_ref, win_ref, wgate_ref, bgate_ref, gcq_ref, wuq_ref,
                    gqn_ref, gqd_ref, gckv_ref, wuk_ref, gkn_ref, gkd_ref, t_ref,
                    gla_ref, ckvn_ref, kr_ref, qh_ref, kh_ref, vbt_ref, qabs_ref):
    u = _rms(h_ref[...], gmix_ref[...]).astype(BF16)
    z = _dot(u, win_ref[...])

    gla_ref[:, GC_Q:GC_K] = z[:, ZC_Q:ZC_K] * (GLA_DK ** -0.5)
    gla_ref[:, GC_K:GC_G] = z[:, ZC_K:ZC_CQ]
    gx = _dot(z[:, ZC_GA:ZC_END].astype(BF16), wgate_ref[...]) + bgate_ref[...]
    log_sig = jnp.minimum(gx, 0.0) - jnp.log(1.0 + jnp.exp(-jnp.abs(gx)))
    gla_ref[:, GC_G:GC_END] = log_sig / GLA_TAU

    ckvn = _rms(z[:, ZC_CKV:ZC_KRD], gckv_ref[...])
    ckvn_ref[...] = ckvn
    cb = ckvn.astype(BF16)
    vbt_ref[0] = ckvn.T.astype(BF16)
    krd = z[:, ZC_KRD:ZC_GA]
    kr_ref[...] = krd[:, 0:QK_ROPE]
    kr_ss = 0.5 * jnp.sum(krd * krd, axis=-1, keepdims=True)
    kn = _dot(cb, wuk_ref[...])

    cqn = _rms(z[:, ZC_CQ:ZC_CKV], gcq_ref[...]).astype(BF16)
    qq = _dot(cqn, wuq_ref[...])

    tab = t_ref[...]
    gqn, gqd, gkn, gkd = gqn_ref[...], gqd_ref[...], gkn_ref[...], gkd_ref[...]
    low = lax.broadcasted_iota(jnp.int32, (1, LANES), 1) < QK_ROPE
    nh = MLA_HEADS * QK_NOPE
    for hh in range(MLA_HEADS):
        lo, hi = hh * QK_NOPE, (hh + 1) * QK_NOPE
        qn = qq[:, lo:hi]
        qd = qq[:, nh + lo:nh + hi]
        ss = jnp.sum(qn * qn, axis=-1, keepdims=True) + 0.5 * jnp.sum(qd * qd, axis=-1, keepdims=True)
        rq = lax.rsqrt(ss / QK_HEAD + EPS) * (QK_SCALE * LOG2E)
        qnope = qn * rq * gqn
        aq = qd * rq * gqd * tab
        qrope = jnp.where(low, aq + pltpu.roll(aq, QK_ROPE, axis=1), 0.0)
        qh_ref[:, hh * HEAD_PAD:hh * HEAD_PAD + QK_NOPE] = qnope.astype(BF16)
        qh_ref[:, hh * HEAD_PAD + QK_NOPE:(hh + 1) * HEAD_PAD] = qrope.astype(BF16)
        qabs = _dot_nt((qnope * gkn).astype(BF16), wuk_ref[:, lo:hi])
        qabs_ref[:, lo:hi] = qabs.astype(BF16)

        knh = kn[:, lo:hi]
        rk = lax.rsqrt((jnp.sum(knh * knh, axis=-1, keepdims=True) + kr_ss) / QK_HEAD + EPS)
        kh_ref[:, hh * HEAD_PAD:hh * HEAD_PAD + QK_NOPE] = (knh * rk * gkn).astype(BF16)
        ak = krd * rk * gkd * tab
        krope = jnp.where(low, ak + pltpu.roll(ak, QK_ROPE, axis=1), 0.0)
        kh_ref[:, hh * HEAD_PAD + QK_NOPE:(hh + 1) * HEAD_PAD] = krope.astype(BF16)


def _mixprep_call(h, gmix, win, wgate, bgate, gcq, wuq, gqn, gqd, gckv, wuk, gkn, gkd, tab, tm):
    n, d = h.shape
    row = lambda w: pl.BlockSpec((tm, w), lambda i: (i, 0))
    consts = [gmix, win, wgate, bgate, gcq, wuq, gqn, gqd, gckv, wuk, gkn, gkd]
    hq = MLA_HEADS * HEAD_PAD
    out_shape = [
        jax.ShapeDtypeStruct((n, GC_END), F32),
        jax.ShapeDtypeStruct((n, KV_LORA), F32),
        jax.ShapeDtypeStruct((n, QK_ROPE), F32),
        jax.ShapeDtypeStruct((n, hq), BF16),
        jax.ShapeDtypeStruct((n, hq), BF16),
        jax.ShapeDtypeStruct((n // tm, KV_LORA, tm), BF16),
        jax.ShapeDtypeStruct((n, MLA_HEADS * KV_LORA), BF16),
    ]
    out_specs = [row(s.shape[1]) for s in out_shape]
    out_specs[5] = pl.BlockSpec((1, KV_LORA, tm), lambda i: (i, 0, 0))
    return pl.pallas_call(
        _mixprep_kernel,
        grid=(n // tm,),
        in_specs=[row(d)] + [_const_spec(c.shape) for c in consts] + [row(LANES)],
        out_specs=out_specs,
        out_shape=out_shape,
        compiler_params=pltpu.CompilerParams(
            dimension_semantics=("arbitrary",), vmem_limit_bytes=VMEM_LIMIT),
        name="mixprep",
    )(h, *consts, tab)


def _gla_out(o, ra, gout):
    outs = []
    for hh in range(GLA_HEADS):
        oh = o[:, hh * GLA_DV:(hh + 1) * GLA_DV]
        outs.append(_rms(oh, gout) * _silu(ra[:, hh * GLA_DV:(hh + 1) * GLA_DV]))
    return outs


def _gla_prompt_kernel(gin_ref, gout_ref, o_ref, st_ref, s_ref, *, tc):
    i = pl.program_id(1)
    c_len = GLA_CHUNK
    kd = GLA_HEADS * GLA_DK
    vd = GLA_HEADS * GLA_DV

    @pl.when(i == 0)
    def _():
        s_ref[...] = jnp.zeros_like(s_ref)

    ri = lax.broadcasted_iota(jnp.int32, (c_len, c_len), 0)
    ci = lax.broadcasted_iota(jnp.int32, (c_len, c_len), 1)
    tri = (ci <= ri).astype(BF16)
    k_head = lax.broadcasted_iota(jnp.int32, (1, kd), 1) // GLA_DK
    v_head = lax.broadcasted_iota(jnp.int32, (1, vd), 1) // GLA_DV
    wide_r = lax.broadcasted_iota(jnp.int32, (c_len, kd), 0)
    wide_s = lax.broadcasted_iota(jnp.int32, (c_len, kd), 1) % c_len
    causal_wide = wide_s <= wide_r
    bd = (lax.broadcasted_iota(jnp.int32, (kd, vd), 0) // GLA_DK
          == lax.broadcasted_iota(jnp.int32, (kd, vd), 1) // GLA_DV)
    gout = gout_ref[...]

    for c in range(tc // c_len):
        rows = slice(c * c_len, (c + 1) * c_len)
        q = gin_ref[rows, GC_Q:GC_K]
        k = gin_ref[rows, GC_K:GC_V]
        v = gin_ref[rows, GC_V:GC_R]
        ra = gin_ref[rows, GC_R:GC_G]
        g = gin_ref[rows, GC_G:GC_END]
        b = _dot3(tri, _split3(g))
        b_last = b[c_len - 1:c_len, :]
        q_t = (q * jnp.exp(b)).astype(BF16)
        k_t = k * jnp.exp(-b)
        k_dec = k * jnp.exp(b_last - b)
        vb = v.astype(BF16)

        kstack = jnp.concatenate(
            [jnp.where(k_head == hh, k_t, 0.0) for hh in range(GLA_HEADS)], axis=0).astype(BF16)
        a = jnp.where(causal_wide, _dot_nt(q_t, kstack), 0.0)
        vbd = jnp.concatenate(
            [jnp.where(v_head == hh, v, 0.0) for hh in range(GLA_HEADS)], axis=0).astype(BF16)
        s_prev = s_ref[...]
        o = _dot(a.astype(BF16), vbd) + _dot(q_t, s_prev.astype(BF16))

        kpad = jnp.concatenate(
            [k_dec, jnp.broadcast_to(b_last, (SUBLANES, kd)),
             jnp.zeros((LANES - c_len - SUBLANES, kd), F32)], axis=0)
        kdt = kpad.T
        decay_col = jnp.exp(kdt[:, c_len:c_len + 1])
        vpad = jnp.concatenate([vb, jnp.zeros((LANES - c_len, vd), BF16)], axis=0)
        ds = _dot(kdt.astype(BF16), vpad)
        s_ref[...] = decay_col * s_prev + jnp.where(bd, ds, 0.0)

        outs = _gla_out(o, ra, gout)
        for hh in range(GLA_HEADS):
            o_ref[rows, hh * GLA_DV:(hh + 1) * GLA_DV] = outs[hh].astype(BF16)

    @pl.when(i == pl.num_programs(1) - 1)
    def _():
        for hh in range(GLA_HEADS):
            st_ref[0, hh] = s_ref[hh * GLA_DK:(hh + 1) * GLA_DK, hh * GLA_DV:(hh + 1) * GLA_DV]


def _gla_prompt_call(gla, gout, batch, seq):
    tc = _pick_tile(seq, (256, 128, 64))
    nc = seq // tc
    vd = GLA_HEADS * GLA_DV
    return pl.pallas_call(
        functools.partial(_gla_prompt_kernel, tc=tc),
        grid=(batch, nc),
        in_specs=[pl.BlockSpec((tc, GC_END), lambda b, i: (b * nc + i, 0)), _const_spec((1, GLA_DV))],
        out_specs=[
            pl.BlockSpec((tc, vd), lambda b, i: (b * nc + i, 0)),
            pl.BlockSpec((1, GLA_HEADS, GLA_DK, GLA_DV), lambda b, i: (b, 0, 0, 0)),
        ],
        out_shape=[
            jax.ShapeDtypeStruct((batch * seq, vd), BF16),
            jax.ShapeDtypeStruct((batch, GLA_HEADS, GLA_DK, GLA_DV), F32),
        ],
        scratch_shapes=[pltpu.VMEM((GLA_HEADS * GLA_DK, vd), F32)],
        compiler_params=pltpu.CompilerParams(
            dimension_semantics=("arbitrary", "arbitrary"), vmem_limit_bytes=VMEM_LIMIT),
        name="gla_prompt",
    )(gla, gout)


def _gla_sample_kernel(gin_ref, s0_ref, gout_ref, o_ref, sn_ref, *, nreq):
    rows = nreq * T_PAD
    kd = GLA_HEADS * GLA_DK
    q = gin_ref[:, GC_Q:GC_K]
    k = gin_ref[:, GC_K:GC_V]
    v = gin_ref[:, GC_V:GC_R]
    ra = gin_ref[:, GC_R:GC_G]
    g = gin_ref[:, GC_G:GC_END]
    ri = lax.broadcasted_iota(jnp.int32, (rows, rows), 0)
    ci = lax.broadcasted_iota(jnp.int32, (rows, rows), 1)
    same = (ri // T_PAD) == (ci // T_PAD)
    causal = same & (ci <= ri)
    g3 = _split3(g)
    b = _dot3(causal.astype(BF16), g3)
    b_last = _dot3(same.astype(BF16), g3)
    q_t = q * jnp.exp(b)
    k_t = (k * jnp.exp(-b)).astype(BF16)
    k_dec = k * jnp.exp(b_last - b)
    vb = v.astype(BF16)
    k_head = lax.broadcasted_iota(jnp.int32, (1, kd), 1) // GLA_DK

    o_intra = []
    for hh in range(GLA_HEADS):
        a = _dot_nt(jnp.where(k_head == hh, q_t, 0.0).astype(BF16), k_t)
        a = jnp.where(causal, a, 0.0).astype(BF16)
        o_intra.append(_dot(a, vb[:, hh * GLA_DV:(hh + 1) * GLA_DV]))

    o_inter = []
    for r in range(nreq):
        qr = q_t[r * T_PAD:(r + 1) * T_PAD, :]
        qm = jnp.concatenate(
            [jnp.where(k_head == hh, qr, 0.0) for hh in range(GLA_HEADS)], axis=0).astype(BF16)
        o_inter.append(_dot(qm, s0_ref[r * kd:(r + 1) * kd, :].astype(BF16)))

    gout = gout_ref[...]
    for hh in range(GLA_HEADS):
        inter = jnp.concatenate(
            [o_inter[r][hh * T_PAD:(hh + 1) * T_PAD, :] for r in range(nreq)], axis=0)
        oh = o_intra[hh] + inter
        y = _rms(oh, gout) * _silu(ra[:, hh * GLA_DV:(hh + 1) * GLA_DV])
        o_ref[:, hh * GLA_DV:(hh + 1) * GLA_DV] = y.astype(BF16)

    kdt = k_dec.T
    blt = b_last.T
    lane = lax.broadcasted_iota(jnp.int32, (1, rows), 1)
    for hh in range(GLA_HEADS):
        kh = kdt[hh * GLA_DK:(hh + 1) * GLA_DK, :]
        bh = blt[hh * GLA_DK:(hh + 1) * GLA_DK, :]
        lhs = jnp.concatenate(
            [jnp.where(lane // T_PAD == r, kh, 0.0) for r in range(nreq)], axis=0).astype(BF16)
        ds = _dot(lhs, vb[:, hh * GLA_DV:(hh + 1) * GLA_DV])
        for r in range(nreq):
            dec = jnp.exp(jnp.sum(jnp.where(lane == r * T_PAD, bh, 0.0), axis=-1, keepdims=True))
            lo = r * kd + hh * GLA_DK
            sn_ref[lo:lo + GLA_DK, :] = dec * s0_ref[lo:lo + GLA_DK, :] + ds[r * GLA_DK:(r + 1) * GLA_DK, :]


def _gla_sample_call(gla_pad, s0, gout, dec_batch):
    nreq = _pick_tile(dec_batch, (16, 8, 4, 2, 1))
    rows = nreq * T_PAD
    kd = GLA_HEADS * GLA_DK
    vd = GLA_HEADS * GLA_DV
    return pl.pallas_call(
        functools.partial(_gla_sample_kernel, nreq=nreq),
        grid=(dec_batch // nreq,),
        in_specs=[
            pl.BlockSpec((rows, GC_END), lambda i: (i, 0)),
            pl.BlockSpec((nreq * kd, GLA_DV), lambda i: (i, 0)),
            _const_spec((1, GLA_DV)),
        ],
        out_specs=[
            pl.BlockSpec((rows, vd), lambda i: (i, 0)),
            pl.BlockSpec((nreq * kd, GLA_DV), lambda i: (i, 0)),
        ],
        out_shape=[
            jax.ShapeDtypeStruct((dec_batch * T_PAD, vd), BF16),
            jax.ShapeDtypeStruct((dec_batch * kd, GLA_DV), F32),
        ],
        compiler_params=pltpu.CompilerParams(
            dimension_semantics=("arbitrary",), vmem_limit_bytes=VMEM_LIMIT),
        name="gla_sample",
    )(gla_pad, s0, gout)


def _mla_prompt_kernel(q_ref, k_ref, vt_ref, wuvt_ref, o_ref, qt_ref, m_ref, l_ref, acc_ref, *, tq):
    i = pl.program_id(1)
    qt_ref[...] = q_ref[...].astype(F32).T.astype(BF16)
    m_ref[...] = jnp.full(m_ref.shape, -jnp.inf, F32)
    l_ref[...] = jnp.zeros_like(l_ref)
    acc_ref[...] = jnp.zeros_like(acc_ref)
    key_i = lax.broadcasted_iota(jnp.int32, (tq, tq), 0)
    qry_i = lax.broadcasted_iota(jnp.int32, (tq, tq), 1)
    diag_ok = key_i <= qry_i

    def block(j, masked):
        k0 = pl.multiple_of(j * tq, tq)
        kblk = k_ref[pl.ds(k0, tq), :]
        ps, alphas = [], []
        for hh in range(MLA_HEADS):
            st = _dot(kblk[:, hh * HEAD_PAD:(hh + 1) * HEAD_PAD], qt_ref[hh * HEAD_PAD:(hh + 1) * HEAD_PAD, :])
            if masked:
                st = jnp.where(diag_ok, st, -jnp.inf)
            m_prev = m_ref[hh:hh + 1, :]
            m_new = jnp.maximum(m_prev, jnp.max(st, axis=0, keepdims=True))
            alpha = jnp.exp2(m_prev - m_new)
            p = jnp.exp2(st - m_new)
            l_ref[hh:hh + 1, :] = alpha * l_ref[hh:hh + 1, :] + jnp.sum(p, axis=0, keepdims=True)
            m_ref[hh:hh + 1, :] = m_new
            ps.append(p.astype(BF16))
            alphas.append(alpha)
        pv = _dot(vt_ref[j], jnp.concatenate(ps, axis=1))
        acc_ref[...] = acc_ref[...] * jnp.concatenate(alphas, axis=1) + pv

    def body(j, carry):
        block(j, False)
        return carry

    lax.fori_loop(0, i, body, 0)
    block(i, True)
    inv_l = 1.0 / jnp.concatenate([l_ref[hh:hh + 1, :] for hh in range(MLA_HEADS)], axis=1)
    lat_t = (acc_ref[...] * inv_l).astype(BF16)
    out_t = jnp.concatenate(
        [_dot(wuvt_ref[hh * V_HEAD:(hh + 1) * V_HEAD, :], lat_t[:, hh * tq:(hh + 1) * tq])
         for hh in range(MLA_HEADS)], axis=0)
    o_ref[...] = out_t.T.astype(BF16)


def _mla_prompt_call(qh, kh, vbt, wuvt, batch, seq, tq):
    nq = seq // tq
    hq = MLA_HEADS * HEAD_PAD
    od = MLA_HEADS * V_HEAD
    return pl.pallas_call(
        functools.partial(_mla_prompt_kernel, tq=tq),
        grid=(batch, nq),
        in_specs=[
            pl.BlockSpec((tq, hq), lambda b, i: (b * nq + i, 0)),
            pl.BlockSpec((seq, hq), lambda b, i: (b, 0)),
            pl.BlockSpec((nq, KV_LORA, tq), lambda b, i: (b, 0, 0)),
            _const_spec((od, KV_LORA)),
        ],
        out_specs=pl.BlockSpec((tq, od), lambda b, i: (b * nq + i, 0)),
        out_shape=jax.ShapeDtypeStruct((batch * seq, od), BF16),
        scratch_shapes=[
            pltpu.VMEM((hq, tq), BF16),
            pltpu.VMEM((SUBLANES, tq), F32),
            pltpu.VMEM((SUBLANES, tq), F32),
            pltpu.VMEM((KV_LORA, MLA_HEADS * tq), F32),
        ],
        compiler_params=pltpu.CompilerParams(
            dimension_semantics=("arbitrary", "arbitrary"), vmem_limit_bytes=VMEM_LIMIT),
        name="mla_prompt",
    )(qh, kh, vbt, wuvt)


def _mla_sample_kernel(pt_ref, qabs_ref, qpe_ref, cnew_ref, krnew_ref, wukt_ref, gpe_ref, cos_ref, sin_ref,
                       wuv_ref, ckv_hbm, krt_hbm, o_ref, cbuf, kbuf, cbf, sc_ref, sem,
                       *, layer, n_pages, page, kc):
    r = pl.program_id(0)
    nreq = pl.num_programs(0)
    slot = r % 2
    past = n_pages * page
    tail = LANES
    ppc = kc // page
    n_chunks = past // kc

    def page_copies(req, sl, c, i):
        p = c * ppc + i
        pg = pt_ref[req * n_pages + p]
        dst = pl.ds(pl.multiple_of(p * page, page), page)
        return (pltpu.make_async_copy(ckv_hbm.at[layer, pg], cbuf.at[sl, dst], sem.at[sl, 0, c]),
                pltpu.make_async_copy(krt_hbm.at[layer, pg], kbuf.at[sl, p], sem.at[sl, 1, c]))

    def issue_chunk(req, sl, c):
        def body(i, carry):
            for cp in page_copies(req, sl, c, i):
                cp.start()
            return carry
        lax.fori_loop(0, ppc, body, 0)

    def wait_chunk(req, sl, c):
        def body(i, carry):
            for cp in page_copies(req, sl, c, i):
                cp.wait()
            return carry
        lax.fori_loop(0, ppc, body, 0)

    @pl.when(r == 0)
    def _():
        for sl in range(2):
            cbuf[sl, past:past + tail, :] = jnp.zeros((tail, KV_LORA), F32)
        for c in range(n_chunks):
            issue_chunk(0, 0, c)

    cbuf[slot, past:past + T_PAD, :] = cnew_ref[0]
    kbuf[slot, n_pages] = krnew_ref[0]

    nrow = MLA_HEADS * T_PAD
    lhs = jnp.concatenate([wukt_ref[...], qabs_ref[0]], axis=0)
    gpe = gpe_ref[...]
    qpe = qpe_ref[0]
    half = QK_ROPE // 2
    nk = MLA_HEADS * QK_NOPE

    def chunk(k0, width, is_tail):
        if not is_tail:
            c = k0 // kc

            @pl.when(r + 1 < nreq)
            def _():
                issue_chunk(r + 1, 1 - slot, c)

            wait_chunk(r, slot, c)
            krt = jnp.concatenate([kbuf[slot, c * ppc + i] for i in range(ppc)], axis=1)
        else:
            krt = kbuf[slot, n_pages]
        cb = cbuf[slot, k0:k0 + width, :].astype(BF16)
        cbf[k0:k0 + width, :] = cb
        res = _dot_nt(lhs, cb)
        ss_kr = jnp.sum(krt * krt, axis=0, keepdims=True)
        y = krt * gpe
        x1, x2 = y[0:half], y[half:QK_ROPE]
        cs, sn = cos_ref[:, k0:k0 + width], sin_ref[:, k0:k0 + width]
        rot = jnp.concatenate([x1 * cs - x2 * sn, x1 * sn + x2 * cs], axis=0).astype(BF16)
        s = res[nk:nk + nrow] + _dot(qpe, rot)
        rks = []
        for hh in range(MLA_HEADS):
            knh = res[hh * QK_NOPE:(hh + 1) * QK_NOPE]
            rk = lax.rsqrt((jnp.sum(knh * knh, axis=0, keepdims=True) + ss_kr) / QK_HEAD + EPS)
            rks.append(jnp.broadcast_to(rk, (T_PAD, width)))
        s = s * jnp.concatenate(rks, axis=0)
        if is_tail:
            tok = lax.broadcasted_iota(jnp.int32, (nrow, width), 0) % T_PAD
            key = lax.broadcasted_iota(jnp.int32, (nrow, width), 1)
            s = jnp.where(key <= tok, s, -jnp.inf)
        sc_ref[:, k0:k0 + width] = s

    for c in range(past // kc):
        chunk(c * kc, kc, False)
    chunk(past, tail, True)

    s_all = sc_ref[...]
    m = jnp.max(s_all, axis=-1, keepdims=True)
    p = jnp.exp2(s_all - m)
    l = jnp.sum(p, axis=-1, keepdims=True)
    lat = _dot(p.astype(BF16), cbf[...]) / l
    for hh in range(MLA_HEADS):
        lh = lat[hh * T_PAD:(hh + 1) * T_PAD, :].astype(BF16)
        o_ref[0, hh * T_PAD:(hh + 1) * T_PAD, :] = _dot(lh, wuv_ref[:, hh * V_HEAD:(hh + 1) * V_HEAD]).astype(BF16)


def _mla_sample_call(page_table, qabs, qpe, cnew, krnew_t, wukt, gpe, cos_t, sin_t, wuv, cache_ckv, cache_krope_t,
                     layer):
    dec_batch, n_pages = page_table.shape
    page = cache_ckv.shape[2]
    assert page == LANES and cache_krope_t.shape[2:] == (QK_ROPE, page)
    past = n_pages * page
    kc = _pick_tile(past, (1024, 512, 256, 128))
    nrow = MLA_HEADS * T_PAD
    total = past + LANES
    req = lambda w: pl.BlockSpec((1, nrow, w), lambda r, pt: (r, 0, 0))
    const = lambda shape: pl.BlockSpec(shape, lambda r, pt: (0,) * len(shape), pipeline_mode=pl.Buffered(1))
    grid_spec = pltpu.PrefetchScalarGridSpec(
        num_scalar_prefetch=1,
        grid=(dec_batch,),
        in_specs=[
            req(KV_LORA), req(QK_ROPE),
            pl.BlockSpec((1, T_PAD, KV_LORA), lambda r, pt: (r, 0, 0)),
            pl.BlockSpec((1, QK_ROPE, LANES), lambda r, pt: (r, 0, 0)),
            const(wukt.shape), const(gpe.shape), const(cos_t.shape), const(sin_t.shape), const(wuv.shape),
            pl.BlockSpec(memory_space=pl.ANY), pl.BlockSpec(memory_space=pl.ANY),
        ],
        out_specs=pl.BlockSpec((1, nrow, V_HEAD), lambda r, pt: (r, 0, 0)),
        scratch_shapes=[
            pltpu.VMEM((2, total, KV_LORA), F32),
            pltpu.VMEM((2, n_pages + 1, QK_ROPE, page), F32),
            pltpu.VMEM((total, KV_LORA), BF16),
            pltpu.VMEM((nrow, total), F32),
            pltpu.SemaphoreType.DMA((2, 2, past // kc)),
        ],
    )
    return pl.pallas_call(
        functools.partial(_mla_sample_kernel, layer=layer, n_pages=n_pages, page=page, kc=kc),
        grid_spec=grid_spec,
        out_shape=jax.ShapeDtypeStruct((dec_batch, nrow, V_HEAD), BF16),
        compiler_params=pltpu.CompilerParams(
            dimension_semantics=("arbitrary",), vmem_limit_bytes=VMEM_LIMIT),
        name="mla_sample",
    )(page_table.reshape(-1), qabs, qpe, cnew, krnew_t, wukt, gpe, cos_t, sin_t, wuv, cache_ckv, cache_krope_t)


def _post_kernel(h_ref, oa_ref, ob_ref, wa_ref, wb_ref, o_ref):
    o_ref[...] = h_ref[...] + _dot(oa_ref[...], wa_ref[...]) + _dot(ob_ref[...], wb_ref[...])


def _post_split_kernel(h_ref, oap_ref, obp_ref, oas_ref, obs_ref, wa_ref, wb_ref, o_ref, *, prompt_tiles):
    i = pl.program_id(0)

    @pl.when(i < prompt_tiles)
    def _():
        _post_kernel(h_ref, oap_ref, obp_ref, wa_ref, wb_ref, o_ref)

    @pl.when(i >= prompt_tiles)
    def _():
        _post_kernel(h_ref, oas_ref, obs_ref, wa_ref, wb_ref, o_ref)


def _post_call(h, oa_p, ob_p, oa_s, ob_s, wa, wb):
    n, d = h.shape
    np_rows, ns_rows = oa_p.shape[0], oa_s.shape[0]
    tm = _pick_tile(ns_rows, (512, 256, 128))
    assert np_rows % tm == 0 and np_rows + ns_rows == n
    pt = np_rows // tm
    row = lambda w: pl.BlockSpec((tm, w), lambda i: (i, 0))
    prow = lambda w: pl.BlockSpec((tm, w), lambda i: (jnp.minimum(i, pt - 1), 0))
    srow = lambda w: pl.BlockSpec((tm, w), lambda i: (jnp.maximum(i - pt, 0), 0))
    wa_w, wb_w = oa_p.shape[1], ob_p.shape[1]
    return pl.pallas_call(
        functools.partial(_post_split_kernel, prompt_tiles=pt),
        grid=(n // tm,),
        in_specs=[row(d), prow(wa_w), prow(wb_w), srow(wa_w), srow(wb_w), _const_spec(wa.shape), _const_spec(wb.shape)],
        out_specs=row(d),
        out_shape=jax.ShapeDtypeStruct((n, d), F32),
        compiler_params=pltpu.CompilerParams(
            dimension_semantics=("arbitrary",), vmem_limit_bytes=VMEM_LIMIT),
        name="post",
    )(h, oa_p, ob_p, oa_s, ob_s, wa, wb)


def _ple_kernel(h_ref, p_ref, g_ref, wg_ref, wp_ref, o_ref):
    h = h_ref[...]
    gate = jax.nn.sigmoid(_dot(_rms(h, g_ref[...]).astype(BF16), wg_ref[...]))
    o_ref[...] = h + gate * _dot(p_ref[...].astype(BF16), wp_ref[...])


def _ple_call(h, p, g, wg, wp):
    n, d = h.shape
    tm = _pick_tile(n, (768, 512, 256, 128))
    row = lambda w: pl.BlockSpec((tm, w), lambda i: (i, 0))
    return pl.pallas_call(
        _ple_kernel,
        grid=(n // tm,),
        in_specs=[row(d), row(p.shape[1]), _const_spec(g.shape), _const_spec(wg.shape), _const_spec(wp.shape)],
        out_specs=row(d),
        out_shape=jax.ShapeDtypeStruct((n, d), F32),
        compiler_params=pltpu.CompilerParams(
            dimension_semantics=("arbitrary",), vmem_limit_bytes=VMEM_LIMIT),
        name="ple",
    )(h, p, g, wg, wp)


def _rope_tables(pos):
    inv = ROPE_THETA ** (-jnp.arange(0, QK_ROPE, 2, dtype=F32) / QK_ROPE)
    ang = pos.astype(F32)[:, None] * inv[None, :]
    return jnp.cos(ang), jnp.sin(ang)


def _dup_rope(x):
    x1, x2 = x[..., :QK_ROPE // 2], x[..., QK_ROPE // 2:]
    return jnp.concatenate([x1, x2, x2, x1], axis=-1)


def kernel(x_prompt, x_sample, cache_ckv, cache_krope, state_gla, page_table, p_prompt, p_sample, g_ffn1, w_ffn1_gu, w_ffn1_d, g_mix, w_in, w_gla_gate, b_gla_gate, g_gla_out, g_cq, w_uq, g_ckv, w_uk, w_uv, g_q, g_k, w_out, g_ffn2, w_ffn2_gu, w_ffn2_d, g_ple, w_ple_gate, w_ple):
    batch, seq, d = x_prompt.shape
    dec_batch, dec_seq, _ = x_sample.shape
    depth = w_in.shape[0]
    n_pages, page = page_table.shape[1], cache_ckv.shape[2]
    past = n_pages * page
    np_rows = batch * seq
    ns_rows = dec_batch * dec_seq
    assert dec_seq <= T_PAD

    kd = GLA_HEADS * GLA_DK
    vd = GLA_HEADS * GLA_DV
    o = 0
    cols = {}
    for name, w in (("q", kd), ("k", kd), ("v", vd), ("r", vd), ("ga", GLA_GATE_RANK), ("cq", Q_LORA),
                    ("ckv", KV_LORA), ("kr", QK_ROPE)):
        cols[name] = w_in[:, :, o:o + w]
        o += w
    win_p = jnp.concatenate(
        [cols["q"], cols["k"], cols["v"], cols["r"], cols["cq"], cols["ckv"], _dup_rope(cols["kr"]), cols["ga"],
         jnp.zeros((depth, d, LANES - GLA_GATE_RANK), F32)], axis=-1).astype(BF16)
    wgate_p = jnp.concatenate(
        [w_gla_gate, jnp.zeros((depth, LANES - GLA_GATE_RANK, kd), F32)], axis=1).astype(BF16)
    wuq4 = w_uq.reshape(depth, Q_LORA, MLA_HEADS, QK_HEAD)
    wuq_p = jnp.concatenate(
        [wuq4[..., :QK_NOPE].reshape(depth, Q_LORA, -1), _dup_rope(wuq4[..., QK_NOPE:]).reshape(depth, Q_LORA, -1)],
        axis=-1).astype(BF16)
    wuk_b = w_uk.astype(BF16)
    wukt_b = jnp.swapaxes(w_uk, 1, 2).astype(BF16)
    wuv_b = w_uv.astype(BF16)
    wuvt_b = jnp.swapaxes(w_uv, 1, 2).astype(BF16)
    wout_b = w_out.astype(BF16)
    wf1gu, wf1d = w_ffn1_gu.astype(BF16), w_ffn1_d.astype(BF16)
    wf2gu, wf2d = w_ffn2_gu.astype(BF16), w_ffn2_d.astype(BF16)
    wpg_b, wp_b = w_ple_gate.astype(BF16), w_ple.astype(BF16)
    row2 = lambda a: a.reshape(depth, 1, -1)
    gqn, gqd = row2(g_q[:, :QK_NOPE]), row2(_dup_rope(g_q[:, QK_NOPE:]))
    gkn, gkd = row2(g_k[:, :QK_NOPE]), row2(_dup_rope(g_k[:, QK_NOPE:]))
    gpe_col = g_k[:, QK_NOPE:].reshape(depth, QK_ROPE, 1)

    pos = jnp.concatenate([jnp.tile(jnp.arange(seq), batch), jnp.tile(past + jnp.arange(dec_seq), dec_batch)])
    cs, sn = _rope_tables(pos)
    tab = jnp.concatenate([cs, cs, -sn, sn], axis=-1)
    cs_k, sn_k = _rope_tables(jnp.arange(past + LANES))
    cos_t, sin_t = cs_k.T, sn_k.T

    x = jnp.concatenate([x_prompt.reshape(np_rows, d), x_sample.reshape(ns_rows, d)], axis=0)
    p_all = jnp.concatenate(
        [p_prompt.reshape(depth, np_rows, -1), p_sample.reshape(depth, ns_rows, -1)], axis=1)
    s0_all = state_gla.reshape(depth, dec_batch * kd, GLA_DV)

    pad_t = lambda a: jnp.pad(a.reshape(dec_batch, dec_seq, -1), ((0, 0), (0, T_PAD - dec_seq), (0, 0)))
    unpad = lambda a: a.reshape(dec_batch, T_PAD, -1)[:, :dec_seq].reshape(ns_rows, -1)

    def heads_rows(a, width):
        a = a.reshape(dec_batch, dec_seq, MLA_HEADS, width).transpose(0, 2, 1, 3)
        a = jnp.pad(a, ((0, 0), (0, 0), (0, T_PAD - dec_seq), (0, 0)))
        return a.reshape(dec_batch, MLA_HEADS * T_PAD, width)

    cache_krope_t = jnp.swapaxes(cache_krope, 2, 3)
    tile = _pick_tile(math.gcd(seq, ns_rows), (512, 256, 128))

    outs = [[] for _ in range(6)]
    for l in range(depth):
        h1 = _ffn_call(x, row2(g_ffn1)[l], wf1gu[l], wf1d[l])
        gla, ckvn, kr, qh, kh, vbt, qabs = _mixprep_call(
            h1, row2(g_mix)[l], win_p[l], wgate_p[l], row2(b_gla_gate)[l], row2(g_cq)[l], wuq_p[l], gqn[l], gqd[l],
            row2(g_ckv)[l], wuk_b[l], gkn[l], gkd[l], tab, tile)
        gout = row2(g_gla_out)[l]

        oa_p, st_p = _gla_prompt_call(gla, gout, batch, seq)
        oa_s, st_s = _gla_sample_call(pad_t(gla[np_rows:]).reshape(dec_batch * T_PAD, -1), s0_all[l], gout, dec_batch)
        ob_p = _mla_prompt_call(qh, kh, vbt, wuvt_b[l], batch, seq, tile)
        qh_s = qh[np_rows:].reshape(ns_rows, MLA_HEADS, HEAD_PAD)[:, :, QK_NOPE:QK_NOPE + QK_ROPE]
        krnew_t = jnp.pad(jnp.swapaxes(kr[np_rows:].reshape(dec_batch, dec_seq, QK_ROPE), 1, 2),
                          ((0, 0), (0, 0), (0, LANES - dec_seq)))
        ob_s = _mla_sample_call(
            page_table, heads_rows(qabs[np_rows:], KV_LORA), heads_rows(qh_s, QK_ROPE), pad_t(ckvn[np_rows:]),
            krnew_t, wukt_b[l], gpe_col[l], cos_t, sin_t, wuv_b[l], cache_ckv, cache_krope_t, l)
        ob_s = ob_s.reshape(dec_batch, MLA_HEADS, T_PAD, V_HEAD)[:, :, :dec_seq].transpose(0, 2, 1, 3)

        h2 = _post_call(h1, oa_p, ob_p, unpad(oa_s), ob_s.reshape(ns_rows, -1), wout_b[l, :vd], wout_b[l, vd:])
        h3 = _ffn_call(h2, row2(g_ffn2)[l], wf2gu[l], wf2d[l])
        x = _ple_call(h3, p_all[l], row2(g_ple)[l], wpg_b[l], wp_b[l])

        outs[0].append(ckvn[:np_rows].reshape(batch, seq, KV_LORA))
        outs[1].append(kr[:np_rows].reshape(batch, seq, QK_ROPE))
        outs[2].append(st_p)
        outs[3].append(ckvn[np_rows:].reshape(dec_batch, dec_seq, KV_LORA))
        outs[4].append(kr[np_rows:].reshape(dec_batch, dec_seq, QK_ROPE))
        outs[5].append(st_s.reshape(dec_batch, GLA_HEADS, GLA_DK, GLA_DV))

    return (x[:np_rows].reshape(batch, seq, d), x[np_rows:].reshape(dec_batch, dec_seq, d),
            *(jnp.stack(o) for o in outs))
```

```python
import functools
import math

import jax
import jax.numpy as jnp
from jax import lax
from jax.experimental import pallas as pl
from jax.experimental.pallas import tpu as pltpu

F32 = jnp.float32
BF16 = jnp.bfloat16

EPS = 1e-6
GLA_HEADS = 4
GLA_DK = 64
GLA_DV = 128
GLA_GATE_RANK = 16
GLA_TAU = 16.0
GLA_CHUNK = 64
MLA_HEADS = 4
Q_LORA = 256
KV_LORA = 128
QK_NOPE = 128
QK_ROPE = 64
QK_HEAD = QK_NOPE + QK_ROPE
V_HEAD = 128
ROPE_THETA = 10000.0
QK_SCALE = QK_HEAD ** -0.5
LOG2E = 1.4426950408889634

LANES = 128
SUBLANES = 8
HEAD_PAD = 256
T_PAD = SUBLANES
VMEM_LIMIT = 56 * 1024 * 1024

ZC_Q, ZC_K, ZC_V, ZC_R, ZC_CQ, ZC_CKV, ZC_KRD, ZC_GA, ZC_END = 0, 256, 512, 1024, 1536, 1792, 1920, 2048, 2176
GC_Q, GC_K, GC_V, GC_R, GC_G, GC_END = 0, 256, 512, 1024, 1536, 1792

NT_DIMS = (((1,), (1,)), ((), ()))


def _const_spec(shape):
    nd = len(shape)
    return pl.BlockSpec(shape, lambda *_: (0,) * nd, pipeline_mode=pl.Buffered(1))


def _pick_tile(n, candidates):
    for c in candidates:
        if n % c == 0:
            return c
    raise ValueError(f"no tile in {candidates} divides {n}")


def _rms(x, g):
    return x * lax.rsqrt(jnp.mean(x * x, axis=-1, keepdims=True) + EPS) * g


def _silu(x):
    return x * jax.nn.sigmoid(x)


def _split3(x):
    hi = x.astype(BF16)
    r1 = x - hi.astype(F32)
    mid = r1.astype(BF16)
    lo = (r1 - mid.astype(F32)).astype(BF16)
    return hi, mid, lo


def _fold_rows(x, op):
    while x.shape[0] > SUBLANES:
        h = x.shape[0] // 2
        x = op(x[:h], x[h:])
    return x


def _sum_rows(x):
    return jnp.sum(_fold_rows(x, jnp.add), axis=0, keepdims=True)


def _max_rows(x):
    return jnp.max(_fold_rows(x, jnp.maximum), axis=0, keepdims=True)


def _dot(a, b):
    return jnp.dot(a, b, preferred_element_type=F32)


def _dot_nt(a, b):
    return lax.dot_general(a, b, NT_DIMS, preferred_element_type=F32)


def _dot3(m, parts):
    return _dot(m, parts[0]) + _dot(m, parts[1]) + _dot(m, parts[2])


def _swiglu_into(acc_ref, xn, wgu_ref, wd_ref, ffn, ch):
    for c in range(ffn // ch):
        gate = _dot(xn, wgu_ref[:, c * ch:(c + 1) * ch])
        up = _dot(xn, wgu_ref[:, ffn + c * ch:ffn + (c + 1) * ch])
        hm = (_silu(gate) * up).astype(BF16)
        part = _dot(hm, wd_ref[c * ch:(c + 1) * ch, :])
        if c == 0:
            acc_ref[...] = part
        else:
            acc_ref[...] += part


def _ffn_kernel(x_ref, g_ref, wgu_ref, wd_ref, o_ref, acc_ref, *, ffn, ch):
    x = x_ref[...]
    _swiglu_into(acc_ref, _rms(x, g_ref[...]).astype(BF16), wgu_ref, wd_ref, ffn, ch)
    o_ref[...] = x + 0.5 * acc_ref[...]


def _ffn_call(x, g, wgu, wd):
    n, d = x.shape
    ffn = wd.shape[0]
    tm = _pick_tile(n, (768, 512, 256, 128))
    ch = _pick_tile(ffn, (256, 128))
    return pl.pallas_call(
        functools.partial(_ffn_kernel, ffn=ffn, ch=ch),
        grid=(n // tm,),
        in_specs=[
            pl.BlockSpec((tm, d), lambda i: (i, 0)),
            _const_spec((1, d)),
            _const_spec((d, 2 * ffn)),
            _const_spec((ffn, d)),
        ],
        out_specs=pl.BlockSpec((tm, d), lambda i: (i, 0)),
        out_shape=jax.ShapeDtypeStruct((n, d), F32),
        scratch_shapes=[pltpu.VMEM((tm, d), F32)],
        compiler_params=pltpu.CompilerParams(
            dimension_semantics=("arbitrary",), vmem_limit_bytes=VMEM_LIMIT),
        name="ffn",
    )(x, g, wgu, wd)


def _mixprep_kernel(h_ref, gmix_ref, win_ref, wgate_ref, bgate_ref, gcq_ref, wuq_ref,
                    gqn_ref, gqd_ref, gckv_ref, wuk_ref, gkn_ref, gkd_ref, t_ref,
                    gla_ref, ckvn_ref, kr_ref, qh_ref, kh_ref, vbt_ref, qabs_ref, *, parts):
    tm = h_ref.shape[0]
    sub = tm // parts
    for part in range(parts):
        rows = slice(part * sub, (part + 1) * sub)
        u = _rms(h_ref[rows, :], gmix_ref[...]).astype(BF16)
        z = _dot(u, win_ref[...])

        gla_ref[rows, GC_Q:GC_K] = z[:, ZC_Q:ZC_K] * (GLA_DK ** -0.5)
        gla_ref[rows, GC_K:GC_G] = z[:, ZC_K:ZC_CQ]
        gx = _dot(z[:, ZC_GA:ZC_END].astype(BF16), wgate_ref[...]) + bgate_ref[...]
        log_sig = jnp.minimum(gx, 0.0) - jnp.log(1.0 + jnp.exp(-jnp.abs(gx)))
        gla_ref[rows, GC_G:GC_END] = log_sig / GLA_TAU

        ckvn = _rms(z[:, ZC_CKV:ZC_KRD], gckv_ref[...])
        ckvn_ref[rows, :] = ckvn
        cb = ckvn.astype(BF16)
        vbt_ref[0, :, rows] = ckvn.T.astype(BF16)
        krd = z[:, ZC_KRD:ZC_GA]
        kr_ref[rows, :] = krd[:, 0:QK_ROPE]
        kr_ss = 0.5 * jnp.sum(krd * krd, axis=-1, keepdims=True)
        kn = _dot(cb, wuk_ref[...])

        cqn = _rms(z[:, ZC_CQ:ZC_CKV], gcq_ref[...]).astype(BF16)
        qq = _dot(cqn, wuq_ref[...])

        tab = t_ref[rows, :]
        gqn, gqd, gkn, gkd = gqn_ref[...], gqd_ref[...], gkn_ref[...], gkd_ref[...]
        low = lax.broadcasted_iota(jnp.int32, (1, LANES), 1) < QK_ROPE
        nh = MLA_HEADS * QK_NOPE
        for hh in range(MLA_HEADS):
            lo, hi = hh * QK_NOPE, (hh + 1) * QK_NOPE
            qn = qq[:, lo:hi]
            qd = qq[:, nh + lo:nh + hi]
            ss = jnp.sum(qn * qn, axis=-1, keepdims=True) + 0.5 * jnp.sum(qd * qd, axis=-1, keepdims=True)
            rq = lax.rsqrt(ss / QK_HEAD + EPS) * (QK_SCALE * LOG2E)
            qnope = qn * rq * gqn
            aq = qd * rq * gqd * tab
            qrope = jnp.where(low, aq + pltpu.roll(aq, QK_ROPE, axis=1), 0.0)
            qh_ref[rows, hh * HEAD_PAD:hh * HEAD_PAD + QK_NOPE] = qnope.astype(BF16)
            qh_ref[rows, hh * HEAD_PAD + QK_NOPE:(hh + 1) * HEAD_PAD] = qrope.astype(BF16)
            qabs = _dot_nt((qnope * gkn).astype(BF16), wuk_ref[:, lo:hi])
            qabs_ref[rows, lo:hi] = qabs.astype(BF16)

            knh = kn[:, lo:hi]
            rk = lax.rsqrt((jnp.sum(knh * knh, axis=-1, keepdims=True) + kr_ss) / QK_HEAD + EPS)
            kh_ref[rows, hh * HEAD_PAD:hh * HEAD_PAD + QK_NOPE] = (knh * rk * gkn).astype(BF16)
            ak = krd * rk * gkd * tab
            krope = jnp.where(low, ak + pltpu.roll(ak, QK_ROPE, axis=1), 0.0)
            kh_ref[rows, hh * HEAD_PAD + QK_NOPE:(hh + 1) * HEAD_PAD] = krope.astype(BF16)


def _mixprep_call(h, gmix, win, wgate, bgate, gcq, wuq, gqn, gqd, gckv, wuk, gkn, gkd, tab, tm):
    n, d = h.shape
    row = lambda w: pl.BlockSpec((tm, w), lambda i: (i, 0))
    consts = [gmix, win, wgate, bgate, gcq, wuq, gqn, gqd, gckv, wuk, gkn, gkd]
    hq = MLA_HEADS * HEAD_PAD
    out_shape = [
        jax.ShapeDtypeStruct((n, GC_END), F32),
        jax.ShapeDtypeStruct((n, KV_LORA), F32),
        jax.ShapeDtypeStruct((n, QK_ROPE), F32),
        jax.ShapeDtypeStruct((n, hq), BF16),
        jax.ShapeDtypeStruct((n, hq), BF16),
        jax.ShapeDtypeStruct((n // tm, KV_LORA, tm), BF16),
        jax.ShapeDtypeStruct((n, MLA_HEADS * KV_LORA), BF16),
    ]
    out_specs = [row(s.shape[1]) for s in out_shape]
    out_specs[5] = pl.BlockSpec((1, KV_LORA, tm), lambda i: (i, 0, 0))
    return pl.pallas_call(
        functools.partial(_mixprep_kernel, parts=2 if tm % (2 * LANES) == 0 else 1),
        grid=(n // tm,),
        in_specs=[row(d)] + [_const_spec(c.shape) for c in consts] + [row(LANES)],
        out_specs=out_specs,
        out_shape=out_shape,
        compiler_params=pltpu.CompilerParams(
            dimension_semantics=("arbitrary",), vmem_limit_bytes=VMEM_LIMIT),
        name="mixprep",
    )(h, *consts, tab)


def _gla_out(o, ra, gout):
    outs = []
    for hh in range(GLA_HEADS):
        oh = o[:, hh * GLA_DV:(hh + 1) * GLA_DV]
        outs.append(_rms(oh, gout) * _silu(ra[:, hh * GLA_DV:(hh + 1) * GLA_DV]))
    return outs


def _gla_prompt_kernel(*refs, tc, nb):
    gin_refs, (gout_ref, o_ref, st_ref, s_ref) = refs[:nb], refs[nb:]
    i = pl.program_id(0)
    c_len = GLA_CHUNK
    kd = GLA_HEADS * GLA_DK
    vd = GLA_HEADS * GLA_DV

    @pl.when(i == 0)
    def _():
        s_ref[...] = jnp.zeros_like(s_ref)

    ri = lax.broadcasted_iota(jnp.int32, (c_len, c_len), 0)
    ci = lax.broadcasted_iota(jnp.int32, (c_len, c_len), 1)
    tri = (ci <= ri).astype(BF16)
    k_head = lax.broadcasted_iota(jnp.int32, (1, kd), 1) // GLA_DK
    v_head = lax.broadcasted_iota(jnp.int32, (1, vd), 1) // GLA_DV
    wide_r = lax.broadcasted_iota(jnp.int32, (c_len, kd), 0)
    wide_s = lax.broadcasted_iota(jnp.int32, (c_len, kd), 1) % c_len
    causal_wide = wide_s <= wide_r
    bd = (lax.broadcasted_iota(jnp.int32, (kd, vd), 0) // GLA_DK
          == lax.broadcasted_iota(jnp.int32, (kd, vd), 1) // GLA_DV)
    gout = gout_ref[...]

    for c, bb in ((c, bb) for c in range(tc // c_len) for bb in range(nb)):
        gin_ref = gin_refs[bb]
        rows = slice(c * c_len, (c + 1) * c_len)
        q = gin_ref[rows, GC_Q:GC_K]
        k = gin_ref[rows, GC_K:GC_V]
        v = gin_ref[rows, GC_V:GC_R]
        ra = gin_ref[rows, GC_R:GC_G]
        g = gin_ref[rows, GC_G:GC_END]
        b = _dot3(tri, _split3(g))
        b_last = b[c_len - 1:c_len, :]
        q_t = (q * jnp.exp(b)).astype(BF16)
        k_t = k * jnp.exp(-b)
        k_dec = k * jnp.exp(b_last - b)
        vb = v.astype(BF16)

        kstack = jnp.concatenate(
            [jnp.where(k_head == hh, k_t, 0.0) for hh in range(GLA_HEADS)], axis=0).astype(BF16)
        a = jnp.where(causal_wide, _dot_nt(q_t, kstack), 0.0)
        vbd = jnp.concatenate(
            [jnp.where(v_head == hh, v, 0.0) for hh in range(GLA_HEADS)], axis=0).astype(BF16)
        s_prev = s_ref[bb]
        o = _dot(a.astype(BF16), vbd) + _dot(q_t, s_prev.astype(BF16))

        kpad = jnp.concatenate(
            [k_dec, jnp.broadcast_to(b_last, (SUBLANES, kd)),
             jnp.zeros((LANES - c_len - SUBLANES, kd), F32)], axis=0)
        kdt = kpad.T
        decay_col = jnp.exp(kdt[:, c_len:c_len + 1])
        vpad = jnp.concatenate([vb, jnp.zeros((LANES - c_len, vd), BF16)], axis=0)
        ds = _dot(kdt.astype(BF16), vpad)
        s_ref[bb] = decay_col * s_prev + jnp.where(bd, ds, 0.0)

        outs = _gla_out(o, ra, gout)
        for hh in range(GLA_HEADS):
            o_ref[bb, rows, hh * GLA_DV:(hh + 1) * GLA_DV] = outs[hh].astype(BF16)

    @pl.when(i == pl.num_programs(0) - 1)
    def _():
        for bb in range(nb):
            for hh in range(GLA_HEADS):
                st_ref[bb, hh] = s_ref[bb, hh * GLA_DK:(hh + 1) * GLA_DK, hh * GLA_DV:(hh + 1) * GLA_DV]


def _gla_prompt_call(gla, gout, batch, seq):
    assert batch <= SUBLANES
    tc = _pick_tile(seq, (256, 128, 64))
    nc = seq // tc
    vd = GLA_HEADS * GLA_DV
    seq_spec = lambda bb: pl.BlockSpec((tc, GC_END), lambda i: (bb * nc + i, 0))
    o, st = pl.pallas_call(
        functools.partial(_gla_prompt_kernel, tc=tc, nb=batch),
        grid=(nc,),
        in_specs=[seq_spec(bb) for bb in range(batch)] + [_const_spec((1, GLA_DV))],
        out_specs=[
            pl.BlockSpec((batch, tc, vd), lambda i: (0, i, 0)),
            pl.BlockSpec((batch, GLA_HEADS, GLA_DK, GLA_DV), lambda i: (0, 0, 0, 0)),
        ],
        out_shape=[
            jax.ShapeDtypeStruct((batch, seq, vd), BF16),
            jax.ShapeDtypeStruct((batch, GLA_HEADS, GLA_DK, GLA_DV), F32),
        ],
        scratch_shapes=[pltpu.VMEM((batch, GLA_HEADS * GLA_DK, vd), F32)],
        compiler_params=pltpu.CompilerParams(
            dimension_semantics=("arbitrary",), vmem_limit_bytes=VMEM_LIMIT),
        name="gla_prompt",
    )(*([gla] * batch), gout)
    return o.reshape(batch * seq, vd), st


def _gla_sample_kernel(gin_ref, s0_ref, gout_ref, o_ref, sn_ref, *, nreq):
    rows = nreq * T_PAD
    kd = GLA_HEADS * GLA_DK
    q = gin_ref[:, GC_Q:GC_K]
    k = gin_ref[:, GC_K:GC_V]
    v = gin_ref[:, GC_V:GC_R]
    ra = gin_ref[:, GC_R:GC_G]
    g = gin_ref[:, GC_G:GC_END]
    ri = lax.broadcasted_iota(jnp.int32, (rows, rows), 0)
    ci = lax.broadcasted_iota(jnp.int32, (rows, rows), 1)
    same = (ri // T_PAD) == (ci // T_PAD)
    causal = same & (ci <= ri)
    g3 = _split3(g)
    b = _dot3(causal.astype(BF16), g3)
    b_last = _dot3(same.astype(BF16), g3)
    q_t = q * jnp.exp(b)
    k_t = (k * jnp.exp(-b)).astype(BF16)
    k_dec = k * jnp.exp(b_last - b)
    vb = v.astype(BF16)
    k_head = lax.broadcasted_iota(jnp.int32, (1, kd), 1) // GLA_DK

    o_intra = []
    for hh in range(GLA_HEADS):
        a = _dot_nt(jnp.where(k_head == hh, q_t, 0.0).astype(BF16), k_t)
        a = jnp.where(causal, a, 0.0).astype(BF16)
        o_intra.append(_dot(a, vb[:, hh * GLA_DV:(hh + 1) * GLA_DV]))

    o_inter = []
    for r in range(nreq):
        qr = q_t[r * T_PAD:(r + 1) * T_PAD, :]
        qm = jnp.concatenate(
            [jnp.where(k_head == hh, qr, 0.0) for hh in range(GLA_HEADS)], axis=0).astype(BF16)
        o_inter.append(_dot(qm, s0_ref[r * kd:(r + 1) * kd, :].astype(BF16)))

    gout = gout_ref[...]
    for hh in range(GLA_HEADS):
        inter = jnp.concatenate(
            [o_inter[r][hh * T_PAD:(hh + 1) * T_PAD, :] for r in range(nreq)], axis=0)
        oh = o_intra[hh] + inter
        y = _rms(oh, gout) * _silu(ra[:, hh * GLA_DV:(hh + 1) * GLA_DV])
        o_ref[:, hh * GLA_DV:(hh + 1) * GLA_DV] = y.astype(BF16)

    kdt = k_dec.T
    blt = b_last.T
    lane = lax.broadcasted_iota(jnp.int32, (1, rows), 1)
    for hh in range(GLA_HEADS):
        kh = kdt[hh * GLA_DK:(hh + 1) * GLA_DK, :]
        bh = blt[hh * GLA_DK:(hh + 1) * GLA_DK, :]
        lhs = jnp.concatenate(
            [jnp.where(lane // T_PAD == r, kh, 0.0) for r in range(nreq)], axis=0).astype(BF16)
        ds = _dot(lhs, vb[:, hh * GLA_DV:(hh + 1) * GLA_DV])
        for r in range(nreq):
            dec = jnp.exp(jnp.sum(jnp.where(lane == r * T_PAD, bh, 0.0), axis=-1, keepdims=True))
            lo = r * kd + hh * GLA_DK
            sn_ref[lo:lo + GLA_DK, :] = dec * s0_ref[lo:lo + GLA_DK, :] + ds[r * GLA_DK:(r + 1) * GLA_DK, :]


def _gla_sample_call(gla_pad, s0, gout, dec_batch):
    nreq = _pick_tile(dec_batch, (16, 8, 4, 2, 1))
    rows = nreq * T_PAD
    kd = GLA_HEADS * GLA_DK
    vd = GLA_HEADS * GLA_DV
    return pl.pallas_call(
        functools.partial(_gla_sample_kernel, nreq=nreq),
        grid=(dec_batch // nreq,),
        in_specs=[
            pl.BlockSpec((rows, GC_END), lambda i: (i, 0)),
            pl.BlockSpec((nreq * kd, GLA_DV), lambda i: (i, 0)),
            _const_spec((1, GLA_DV)),
        ],
        out_specs=[
            pl.BlockSpec((rows, vd), lambda i: (i, 0)),
            pl.BlockSpec((nreq * kd, GLA_DV), lambda i: (i, 0)),
        ],
        out_shape=[
            jax.ShapeDtypeStruct((dec_batch * T_PAD, vd), BF16),
            jax.ShapeDtypeStruct((dec_batch * kd, GLA_DV), F32),
        ],
        compiler_params=pltpu.CompilerParams(
            dimension_semantics=("arbitrary",), vmem_limit_bytes=VMEM_LIMIT),
        name="gla_sample",
    )(gla_pad, s0, gout)


def _mla_prompt_kernel(q_ref, k_ref, vt_ref, wuvt_ref, o_ref, qt_ref, m_ref, l_ref, acc_ref, *, tq):
    i = pl.program_id(1)
    qt_ref[...] = q_ref[...].astype(F32).T.astype(BF16)
    m_ref[...] = jnp.full(m_ref.shape, -jnp.inf, F32)
    l_ref[...] = jnp.zeros_like(l_ref)
    acc_ref[...] = jnp.zeros_like(acc_ref)
    key_i = lax.broadcasted_iota(jnp.int32, (tq, tq), 0)
    qry_i = lax.broadcasted_iota(jnp.int32, (tq, tq), 1)
    diag_ok = key_i <= qry_i

    def block(j, masked):
        k0 = pl.multiple_of(j * tq, tq)
        kblk = k_ref[pl.ds(k0, tq), :]
        ps, alphas = [], []
        for hh in range(MLA_HEADS):
            st = _dot(kblk[:, hh * HEAD_PAD:(hh + 1) * HEAD_PAD], qt_ref[hh * HEAD_PAD:(hh + 1) * HEAD_PAD, :])
            if masked:
                st = jnp.where(diag_ok, st, -jnp.inf)
            m_prev = m_ref[hh:hh + 1, :]
            m_new = jnp.maximum(m_prev, _max_rows(st))
            alpha = jnp.exp2(m_prev - m_new)
            p = jnp.exp2(st - m_new)
            l_ref[hh:hh + 1, :] = alpha * l_ref[hh:hh + 1, :] + _sum_rows(p)
            m_ref[hh:hh + 1, :] = m_new
            ps.append(p.astype(BF16))
            alphas.append(alpha)
        pv = _dot(vt_ref[j], jnp.concatenate(ps, axis=1))
        acc_ref[...] = acc_ref[...] * jnp.concatenate(alphas, axis=1) + pv

    def body(j, carry):
        block(j, False)
        return carry

    lax.fori_loop(0, i, body, 0)
    block(i, True)
    inv_l = 1.0 / jnp.concatenate([l_ref[hh:hh + 1, :] for hh in range(MLA_HEADS)], axis=1)
    lat_t = (acc_ref[...] * inv_l).astype(BF16)
    out_t = jnp.concatenate(
        [_dot(wuvt_ref[hh * V_HEAD:(hh + 1) * V_HEAD, :], lat_t[:, hh * tq:(hh + 1) * tq])
         for hh in range(MLA_HEADS)], axis=0)
    o_ref[...] = out_t.T.astype(BF16)


def _mla_prompt_call(qh, kh, vbt, wuvt, batch, seq, tq):
    nq = seq // tq
    hq = MLA_HEADS * HEAD_PAD
    od = MLA_HEADS * V_HEAD
    return pl.pallas_call(
        functools.partial(_mla_prompt_kernel, tq=tq),
        grid=(batch, nq),
        in_specs=[
            pl.BlockSpec((tq, hq), lambda b, i: (b * nq + i, 0)),
            pl.BlockSpec((seq, hq), lambda b, i: (b, 0)),
            pl.BlockSpec((nq, KV_LORA, tq), lambda b, i: (b, 0, 0)),
            _const_spec((od, KV_LORA)),
        ],
        out_specs=pl.BlockSpec((tq, od), lambda b, i: (b * nq + i, 0)),
        out_shape=jax.ShapeDtypeStruct((batch * seq, od), BF16),
        scratch_shapes=[
            pltpu.VMEM((hq, tq), BF16),
            pltpu.VMEM((SUBLANES, tq), F32),
            pltpu.VMEM((SUBLANES, tq), F32),
            pltpu.VMEM((KV_LORA, MLA_HEADS * tq), F32),
        ],
        compiler_params=pltpu.CompilerParams(
            dimension_semantics=("arbitrary", "arbitrary"), vmem_limit_bytes=VMEM_LIMIT),
        name="mla_prompt",
    )(qh, kh, vbt, wuvt)


def _mla_sample_kernel(pt_ref, qabs_ref, qpe_ref, cnew_ref, krnew_ref, wukt_ref, gpe_ref, cos_ref, sin_ref,
                       wuv_ref, ckv_hbm, krt_hbm, o_ref, cbuf, kbuf, cbf, sc_ref, rt_ref, sem,
                       *, layer, n_pages, page, kc, tw):
    r = pl.program_id(0)
    nreq = pl.num_programs(0)
    slot = r % 2
    past = n_pages * page
    tail = LANES
    ppc = kc // page
    n_chunks = past // kc

    def page_copies(req, sl, c, i):
        p = c * ppc + i
        pg = pt_ref[req * n_pages + p]
        dst = pl.ds(p * page, page) if isinstance(p, int) else pl.ds(pl.multiple_of(p * page, page), page)
        return (pltpu.make_async_copy(ckv_hbm.at[layer, pg], cbuf.at[sl, dst], sem.at[sl, 0, c]),
                pltpu.make_async_copy(krt_hbm.at[layer, pg], kbuf.at[sl, p], sem.at[sl, 1, c]))

    half = QK_ROPE // 2
    nxt = jnp.minimum(r + 1, nreq - 1)

    @pl.when(r == 0)
    def _():
        for sl in range(2):
            cbuf[sl, past:past + tail, :] = jnp.zeros((tail, KV_LORA), F32)

        def first(p, carry):
            for c in range(n_chunks):
                for cp in page_copies(0, 0, c, p):
                    cp.start()
            return carry
        lax.fori_loop(0, ppc, first, 0)
        g1, g2 = gpe_ref[0:half, :], gpe_ref[half:QK_ROPE, :]
        rt_ref[0] = cos_ref[...] * g1
        rt_ref[1] = sin_ref[...] * g2
        rt_ref[2] = sin_ref[...] * g1
        rt_ref[3] = cos_ref[...] * g2

    cbuf[slot, past:past + T_PAD, :] = cnew_ref[0]
    kbuf[slot, n_pages] = krnew_ref[0]

    nrow = MLA_HEADS * T_PAD
    lhs = jnp.concatenate([wukt_ref[...], qabs_ref[0]], axis=0)
    qpe = qpe_ref[0]
    nk = MLA_HEADS * QK_NOPE

    def features(k0, width):
        cb = cbuf[slot, k0:k0 + width, :].astype(BF16)
        cbf[k0:k0 + width, :] = cb
        return _dot_nt(lhs, cb)

    def scores(res_rows, krt, k0, width, is_tail):
        ss_kr = _sum_rows(krt * krt)
        x1, x2 = krt[0:half], krt[half:QK_ROPE]
        cols = slice(k0, k0 + width)
        rot = jnp.concatenate([x1 * rt_ref[0, :, cols] - x2 * rt_ref[1, :, cols],
                               x1 * rt_ref[2, :, cols] + x2 * rt_ref[3, :, cols]], axis=0).astype(BF16)
        s = res_rows(nk, nk + nrow) + _dot(qpe, rot)
        rks = []
        for hh in range(MLA_HEADS):
            knh = res_rows(hh * QK_NOPE, (hh + 1) * QK_NOPE)
            rk = lax.rsqrt((_sum_rows(knh * knh) + ss_kr) / QK_HEAD + EPS)
            rks.append(jnp.broadcast_to(rk, (T_PAD, width)))
        s = s * jnp.concatenate(rks, axis=0)
        if is_tail:
            tok = lax.broadcasted_iota(jnp.int32, (nrow, width), 0) % T_PAD
            key = lax.broadcasted_iota(jnp.int32, (nrow, width), 1)
            s = jnp.where(key <= tok, s, -jnp.inf)
        sc_ref[:, cols] = s

    ppt = tw // page
    for c in range(n_chunks):
        for i in range(ppc):
            for cp in page_copies(nxt, 1 - slot, c, i):
                cp.start()
        for i in range(ppc):
            for cp in page_copies(r, slot, c, i):
                cp.wait()
        for t in range(kc // tw):
            k0 = c * kc + t * tw
            res = features(k0, tw)
            krt = jnp.concatenate([kbuf[slot, k0 // page + i] for i in range(ppt)], axis=1)
            scores(lambda lo, hi: res[lo:hi], krt, k0, tw, False)
    res_tail = features(past, tail)
    scores(lambda lo, hi: res_tail[lo:hi], kbuf[slot, n_pages], past, tail, True)

    s_all = sc_ref[...]
    m = jnp.max(s_all, axis=-1, keepdims=True)
    p = jnp.exp2(s_all - m)
    l = jnp.sum(p, axis=-1, keepdims=True)
    lat = _dot(p.astype(BF16), cbf[...]) / l
    for hh in range(MLA_HEADS):
        lh = lat[hh * T_PAD:(hh + 1) * T_PAD, :].astype(BF16)
        o_ref[0, hh * T_PAD:(hh + 1) * T_PAD, :] = _dot(lh, wuv_ref[:, hh * V_HEAD:(hh + 1) * V_HEAD]).astype(BF16)

    @pl.when(r == nreq - 1)
    def _():
        def drain(i, carry):
            for c in range(n_chunks):
                for cp in page_copies(nxt, 1 - slot, c, i):
                    cp.wait()
            return carry
        lax.fori_loop(0, ppc, drain, 0)


def _mla_sample_call(page_table, qabs, qpe, cnew, krnew_t, wukt, gpe, cos_t, sin_t, wuv, cache_ckv, cache_krope_t,
                     layer):
    dec_batch, n_pages = page_table.shape
    page = cache_ckv.shape[2]
    assert page == LANES and cache_krope_t.shape[2:] == (QK_ROPE, page)
    past = n_pages * page
    kc = _pick_tile(past // 2, (4096, 2048, 1024, 512, 256, 128))
    nrow = MLA_HEADS * T_PAD
    total = past + LANES
    req = lambda w: pl.BlockSpec((1, nrow, w), lambda r, pt: (r, 0, 0))
    const = lambda shape: pl.BlockSpec(shape, lambda r, pt: (0,) * len(shape), pipeline_mode=pl.Buffered(1))
    grid_spec = pltpu.PrefetchScalarGridSpec(
        num_scalar_prefetch=1,
        grid=(dec_batch,),
        in_specs=[
            req(KV_LORA), req(QK_ROPE),
            pl.BlockSpec((1, T_PAD, KV_LORA), lambda r, pt: (r, 0, 0)),
            pl.BlockSpec((1, QK_ROPE, LANES), lambda r, pt: (r, 0, 0)),
            const(wukt.shape), const(gpe.shape), const(cos_t.shape), const(sin_t.shape), const(wuv.shape),
            pl.BlockSpec(memory_space=pl.ANY), pl.BlockSpec(memory_space=pl.ANY),
        ],
        out_specs=pl.BlockSpec((1, nrow, V_HEAD), lambda r, pt: (r, 0, 0)),
        scratch_shapes=[
            pltpu.VMEM((2, total, KV_LORA), F32),
            pltpu.VMEM((2, n_pages + 1, QK_ROPE, page), F32),
            pltpu.VMEM((total, KV_LORA), BF16),
            pltpu.VMEM((nrow, total), F32),
            pltpu.VMEM((4, QK_ROPE // 2, total), F32),
            pltpu.SemaphoreType.DMA((2, 2, past // kc)),
        ],
    )
    return pl.pallas_call(
        functools.partial(_mla_sample_kernel, layer=layer, n_pages=n_pages, page=page, kc=kc, tw=kc),
        grid_spec=grid_spec,
        out_shape=jax.ShapeDtypeStruct((dec_batch, nrow, V_HEAD), BF16),
        compiler_params=pltpu.CompilerParams(
            dimension_semantics=("arbitrary",), vmem_limit_bytes=VMEM_LIMIT),
        name="mla_sample",
    )(page_table.reshape(-1), qabs, qpe, cnew, krnew_t, wukt, gpe, cos_t, sin_t, wuv, cache_ckv, cache_krope_t)


def _mid_kernel(h_ref, oap_ref, obp_ref, oas_ref, obs_ref, wa_ref, wb_ref, g2_ref, wgu_ref, wd_ref,
                p_ref, gp_ref, wpg_ref, wp_ref, o_ref, x_ref, acc_ref, *, ffn, ch, prompt_tiles):
    i = pl.program_id(0)

    def out_proj(oa_ref, ob_ref):
        x_ref[...] = h_ref[...] + _dot(oa_ref[...], wa_ref[...]) + _dot(ob_ref[...], wb_ref[...])

    @pl.when(i < prompt_tiles)
    def _():
        out_proj(oap_ref, obp_ref)

    @pl.when(i >= prompt_tiles)
    def _():
        out_proj(oas_ref, obs_ref)

    x = x_ref[...]
    _swiglu_into(acc_ref, _rms(x, g2_ref[...]).astype(BF16), wgu_ref, wd_ref, ffn, ch)
    x = x + 0.5 * acc_ref[...]
    gate = jax.nn.sigmoid(_dot(_rms(x, gp_ref[...]).astype(BF16), wpg_ref[...]))
    o_ref[...] = x + gate * _dot(p_ref[...].astype(BF16), wp_ref[...])


def _mid_call(h, oa_p, ob_p, oa_s, ob_s, wa, wb, g2, wgu, wd, p, gp, wpg, wp):
    n, d = h.shape
    ffn = wd.shape[0]
    np_rows, ns_rows = oa_p.shape[0], oa_s.shape[0]
    tm = _pick_tile(ns_rows, (512, 256, 128))
    assert np_rows % tm == 0 and np_rows + ns_rows == n
    pt = np_rows // tm
    ch = _pick_tile(ffn, (256, 128))
    row = lambda w: pl.BlockSpec((tm, w), lambda i: (i, 0))
    prow = lambda w: pl.BlockSpec((tm, w), lambda i: (jnp.minimum(i, pt - 1), 0))
    srow = lambda w: pl.BlockSpec((tm, w), lambda i: (jnp.maximum(i - pt, 0), 0))
    wa_w, wb_w = oa_p.shape[1], ob_p.shape[1]
    consts = [wa, wb, g2, wgu, wd]
    consts2 = [gp, wpg, wp]
    return pl.pallas_call(
        functools.partial(_mid_kernel, ffn=ffn, ch=ch, prompt_tiles=pt),
        grid=(n // tm,),
        in_specs=([row(d), prow(wa_w), prow(wb_w), srow(wa_w), srow(wb_w)] + [_const_spec(c.shape) for c in consts]
                  + [row(p.shape[1])] + [_const_spec(c.shape) for c in consts2]),
        out_specs=row(d),
        out_shape=jax.ShapeDtypeStruct((n, d), F32),
        scratch_shapes=[pltpu.VMEM((tm, d), F32), pltpu.VMEM((tm, d), F32)],
        compiler_params=pltpu.CompilerParams(
            dimension_semantics=("arbitrary",), vmem_limit_bytes=VMEM_LIMIT),
        name="mid",
    )(h, oa_p, ob_p, oa_s, ob_s, *consts, p, *consts2)


def _rope_tables(pos):
    inv = ROPE_THETA ** (-jnp.arange(0, QK_ROPE, 2, dtype=F32) / QK_ROPE)
    ang = pos.astype(F32)[:, None] * inv[None, :]
    return jnp.cos(ang), jnp.sin(ang)


def _dup_rope(x):
    x1, x2 = x[..., :QK_ROPE // 2], x[..., QK_ROPE // 2:]
    return jnp.concatenate([x1, x2, x2, x1], axis=-1)


def kernel(x_prompt, x_sample, cache_ckv, cache_krope, state_gla, page_table, p_prompt, p_sample, g_ffn1, w_ffn1_gu, w_ffn1_d, g_mix, w_in, w_gla_gate, b_gla_gate, g_gla_out, g_cq, w_uq, g_ckv, w_uk, w_uv, g_q, g_k, w_out, g_ffn2, w_ffn2_gu, w_ffn2_d, g_ple, w_ple_gate, w_ple):
    batch, seq, d = x_prompt.shape
    dec_batch, dec_seq, _ = x_sample.shape
    depth = w_in.shape[0]
    n_pages, page = page_table.shape[1], cache_ckv.shape[2]
    past = n_pages * page
    np_rows = batch * seq
    ns_rows = dec_batch * dec_seq
    assert dec_seq <= T_PAD

    kd = GLA_HEADS * GLA_DK
    vd = GLA_HEADS * GLA_DV
    o = 0
    cols = {}
    for name, w in (("q", kd), ("k", kd), ("v", vd), ("r", vd), ("ga", GLA_GATE_RANK), ("cq", Q_LORA),
                    ("ckv", KV_LORA), ("kr", QK_ROPE)):
        cols[name] = w_in[:, :, o:o + w]
        o += w
    win_p = jnp.concatenate(
        [cols["q"], cols["k"], cols["v"], cols["r"], cols["cq"], cols["ckv"], _dup_rope(cols["kr"]), cols["ga"],
         jnp.zeros((depth, d, LANES - GLA_GATE_RANK), F32)], axis=-1).astype(BF16)
    wgate_p = jnp.concatenate(
        [w_gla_gate, jnp.zeros((depth, LANES - GLA_GATE_RANK, kd), F32)], axis=1).astype(BF16)
    wuq4 = w_uq.reshape(depth, Q_LORA, MLA_HEADS, QK_HEAD)
    wuq_p = jnp.concatenate(
        [wuq4[..., :QK_NOPE].reshape(depth, Q_LORA, -1), _dup_rope(wuq4[..., QK_NOPE:]).reshape(depth, Q_LORA, -1)],
        axis=-1).astype(BF16)
    wuk_b = w_uk.astype(BF16)
    wukt_b = jnp.swapaxes(w_uk, 1, 2).astype(BF16)
    wuv_b = w_uv.astype(BF16)
    wuvt_b = jnp.swapaxes(w_uv, 1, 2).astype(BF16)
    wout_b = w_out.astype(BF16)
    wf1gu, wf1d = w_ffn1_gu.astype(BF16), w_ffn1_d.astype(BF16)
    wf2gu, wf2d = w_ffn2_gu.astype(BF16), w_ffn2_d.astype(BF16)
    wpg_b, wp_b = w_ple_gate.astype(BF16), w_ple.astype(BF16)
    row2 = lambda a: a.reshape(depth, 1, -1)
    gqn, gqd = row2(g_q[:, :QK_NOPE]), row2(_dup_rope(g_q[:, QK_NOPE:]))
    gkn, gkd = row2(g_k[:, :QK_NOPE]), row2(_dup_rope(g_k[:, QK_NOPE:]))
    gpe_col = g_k[:, QK_NOPE:].reshape(depth, QK_ROPE, 1)

    pos = jnp.concatenate([jnp.tile(jnp.arange(seq), batch), jnp.tile(past + jnp.arange(dec_seq), dec_batch)])
    cs, sn = _rope_tables(pos)
    tab = jnp.concatenate([cs, cs, -sn, sn], axis=-1)
    cs_k, sn_k = _rope_tables(jnp.arange(past + LANES))
    cos_t, sin_t = cs_k.T, sn_k.T

    x = jnp.concatenate([x_prompt.reshape(np_rows, d), x_sample.reshape(ns_rows, d)], axis=0)
    p_all = jnp.concatenate(
        [p_prompt.reshape(depth, np_rows, -1), p_sample.reshape(depth, ns_rows, -1)], axis=1)
    s0_all = state_gla.reshape(depth, dec_batch * kd, GLA_DV)

    pad_t = lambda a: jnp.pad(a.reshape(dec_batch, dec_seq, -1), ((0, 0), (0, T_PAD - dec_seq), (0, 0)))
    unpad = lambda a: a.reshape(dec_batch, T_PAD, -1)[:, :dec_seq].reshape(ns_rows, -1)

    def heads_rows(a, width):
        a = a.reshape(dec_batch, dec_seq, MLA_HEADS, width).transpose(0, 2, 1, 3)
        a = jnp.pad(a, ((0, 0), (0, 0), (0, T_PAD - dec_seq), (0, 0)))
        return a.reshape(dec_batch, MLA_HEADS * T_PAD, width)

    cache_krope_t = jnp.swapaxes(cache_krope, 2, 3)
    tile = _pick_tile(math.gcd(seq, ns_rows), (512, 256, 128))

    outs = [[] for _ in range(6)]
    for l in range(depth):
        h1 = _ffn_call(x, row2(g_ffn1)[l], wf1gu[l], wf1d[l])
        gla, ckvn, kr, qh, kh, vbt, qabs = _mixprep_call(
            h1, row2(g_mix)[l], win_p[l], wgate_p[l], row2(b_gla_gate)[l], row2(g_cq)[l], wuq_p[l], gqn[l], gqd[l],
            row2(g_ckv)[l], wuk_b[l], gkn[l], gkd[l], tab, tile)
        gout = row2(g_gla_out)[l]

        oa_p, st_p = _gla_prompt_call(gla, gout, batch, seq)
        oa_s, st_s = _gla_sample_call(pad_t(gla[np_rows:]).reshape(dec_batch * T_PAD, -1), s0_all[l], gout, dec_batch)
        ob_p = _mla_prompt_call(qh, kh, vbt, wuvt_b[l], batch, seq, tile)
        qh_s = qh[np_rows:].reshape(ns_rows, MLA_HEADS, HEAD_PAD)[:, :, QK_NOPE:QK_NOPE + QK_ROPE]
        krnew_t = jnp.pad(jnp.swapaxes(kr[np_rows:].reshape(dec_batch, dec_seq, QK_ROPE), 1, 2),
                          ((0, 0), (0, 0), (0, LANES - dec_seq)))
        ob_s = _mla_sample_call(
            page_table, heads_rows(qabs[np_rows:], KV_LORA), heads_rows(qh_s, QK_ROPE), pad_t(ckvn[np_rows:]),
            krnew_t, wukt_b[l], gpe_col[l], cos_t, sin_t, wuv_b[l], cache_ckv, cache_krope_t, l)
        ob_s = ob_s.reshape(dec_batch, MLA_HEADS, T_PAD, V_HEAD)[:, :, :dec_seq].transpose(0, 2, 1, 3)

        x = _mid_call(h1, oa_p, ob_p, unpad(oa_s), ob_s.reshape(ns_rows, -1), wout_b[l, :vd], wout_b[l, vd:],
                      row2(g_ffn2)[l], wf2gu[l], wf2d[l], p_all[l], row2(g_ple)[l], wpg_b[l], wp_b[l])

        outs[0].append(ckvn[:np_rows].reshape(batch, seq, KV_LORA))
        outs[1].append(kr[:np_rows].reshape(batch, seq, QK_ROPE))
        outs[2].append(st_p)
        outs[3].append(ckvn[np_rows:].reshape(dec_batch, dec_seq, KV_LORA))
        outs[4].append(kr[np_rows:].reshape(dec_batch, dec_seq, QK_ROPE))
        outs[5].append(st_s.reshape(dec_batch, GLA_HEADS, GLA_DK, GLA_DV))

    return (x[:np_rows].reshape(batch, seq, d), x[np_rows:].reshape(dec_batch, dec_seq, d),
            *(jnp.stack(o) for o in outs))
```

```python
import functools
import math

import jax
import jax.numpy as jnp
from jax import lax
from jax.experimental import pallas as pl
from jax.experimental.pallas import tpu as pltpu

F32 = jnp.float32
BF16 = jnp.bfloat16

EPS = 1e-6
GLA_HEADS = 4
GLA_DK = 64
GLA_DV = 128
GLA_GATE_RANK = 16
GLA_TAU = 16.0
GLA_CHUNK = 64
MLA_HEADS = 4
Q_LORA = 256
KV_LORA = 128
QK_NOPE = 128
QK_ROPE = 64
QK_HEAD = QK_NOPE + QK_ROPE
V_HEAD = 128
ROPE_THETA = 10000.0
QK_SCALE = QK_HEAD ** -0.5
LOG2E = 1.4426950408889634

LANES = 128
SUBLANES = 8
HEAD_PAD = 256
T_PAD = SUBLANES
VMEM_LIMIT = 56 * 1024 * 1024

ZC_Q, ZC_K, ZC_V, ZC_R, ZC_CQ, ZC_CKV, ZC_KRD, ZC_GA, ZC_END = 0, 256, 512, 1024, 1536, 1792, 1920, 2048, 2176
GC_Q, GC_K, GC_V, GC_R, GC_G, GC_END = 0, 256, 512, 1024, 1536, 1792

NT_DIMS = (((1,), (1,)), ((), ()))


def _layer_spec(arr, layer):
    shape = arr.shape[1:]
    nd = len(shape)
    return pl.BlockSpec((None,) + shape, lambda *_: (layer,) + (0,) * nd, pipeline_mode=pl.Buffered(1))


def _split_rows(tm, width, prompt_tiles):
    first = pl.BlockSpec((tm, width), lambda i: (jnp.minimum(i, prompt_tiles - 1), 0))
    second = pl.BlockSpec((tm, width), lambda i: (jnp.maximum(i - prompt_tiles, 0), 0))
    return first, second


def _pick_tile(n, candidates):
    for c in candidates:
        if n % c == 0:
            return c
    raise ValueError(f"no tile in {candidates} divides {n}")


def _rms(x, g):
    return x * lax.rsqrt(jnp.mean(x * x, axis=-1, keepdims=True) + EPS) * g


def _silu(x):
    return x * jax.nn.sigmoid(x)


def _split3(x):
    hi = x.astype(BF16)
    r1 = x - hi.astype(F32)
    mid = r1.astype(BF16)
    lo = (r1 - mid.astype(F32)).astype(BF16)
    return hi, mid, lo


def _fold_rows(x, op):
    while x.shape[0] > SUBLANES:
        h = x.shape[0] // 2
        x = op(x[:h], x[h:])
    return x


def _sum_rows(x):
    return jnp.sum(_fold_rows(x, jnp.add), axis=0, keepdims=True)


def _dot(a, b):
    return jnp.dot(a, b, preferred_element_type=F32)


def _dot_nt(a, b):
    return lax.dot_general(a, b, NT_DIMS, preferred_element_type=F32)


def _dot3(m, parts):
    return _dot(m, parts[0]) + _dot(m, parts[1]) + _dot(m, parts[2])


def _swiglu_into(acc_ref, xn, wgu_ref, wd_ref, ffn, ch):
    for c in range(ffn // ch):
        gate = _dot(xn, wgu_ref[:, c * ch:(c + 1) * ch])
        up = _dot(xn, wgu_ref[:, ffn + c * ch:ffn + (c + 1) * ch])
        hm = (_silu(gate) * up).astype(BF16)
        part = _dot(hm, wd_ref[c * ch:(c + 1) * ch, :])
        if c == 0:
            acc_ref[...] = part
        else:
            acc_ref[...] += part


def _ffn_kernel(x_ref, g_ref, wgu_ref, wd_ref, o_ref, acc_ref, *, ffn, ch):
    x = x_ref[...]
    _swiglu_into(acc_ref, _rms(x, g_ref[...]).astype(BF16), wgu_ref, wd_ref, ffn, ch)
    o_ref[...] = x + 0.5 * acc_ref[...]


def _ffn_split_kernel(xp_ref, xs_ref, g_ref, wgu_ref, wd_ref, o_ref, acc_ref, x_ref, *, ffn, ch, prompt_tiles):
    i = pl.program_id(0)

    @pl.when(i < prompt_tiles)
    def _():
        x_ref[...] = xp_ref[...]

    @pl.when(i >= prompt_tiles)
    def _():
        x_ref[...] = xs_ref[...]

    _ffn_kernel(x_ref, g_ref, wgu_ref, wd_ref, o_ref, acc_ref, ffn=ffn, ch=ch)


def _ffn_call(x, g, wgu, wd, layer):
    ffn = wd.shape[1]
    ch = _pick_tile(ffn, (256, 128))
    consts = [g, wgu, wd]
    if isinstance(x, tuple):
        xp, xs = x
        d = xp.shape[1]
        n = xp.shape[0] + xs.shape[0]
        tm = _pick_tile(math.gcd(xp.shape[0], xs.shape[0]), (512, 256, 128))
        pt = xp.shape[0] // tm
        body = functools.partial(_ffn_split_kernel, ffn=ffn, ch=ch, prompt_tiles=pt)
        x_specs, xs_args = list(_split_rows(tm, d, pt)), [xp, xs]
    else:
        n, d = x.shape
        tm = _pick_tile(n, (768, 512, 256, 128))
        body = functools.partial(_ffn_kernel, ffn=ffn, ch=ch)
        x_specs, xs_args = [pl.BlockSpec((tm, d), lambda i: (i, 0))], [x]
    return pl.pallas_call(
        body,
        grid=(n // tm,),
        in_specs=x_specs + [_layer_spec(c, layer) for c in consts],
        out_specs=pl.BlockSpec((tm, d), lambda i: (i, 0)),
        out_shape=jax.ShapeDtypeStruct((n, d), F32),
        scratch_shapes=[pltpu.VMEM((tm, d), F32)] * len(xs_args),
        compiler_params=pltpu.CompilerParams(
            dimension_semantics=("arbitrary",), vmem_limit_bytes=VMEM_LIMIT),
        name="ffn",
    )(*xs_args, *consts)


def _mixprep_kernel(h_ref, gmix_ref, win_ref, wgate_ref, bgate_ref, gcq_ref, wuq_ref,
                    gqn_ref, gqd_ref, gckv_ref, wuk_ref, gkn_ref, gkd_ref, t_ref,
                    gla_ref, ckvn_ref, kr_ref, qh_ref, kh_ref, vbt_ref, qabs_ref, *, parts):
    tm = h_ref.shape[0]
    sub = tm // parts
    for part in range(parts):
        rows = slice(part * sub, (part + 1) * sub)
        u = _rms(h_ref[rows, :], gmix_ref[...]).astype(BF16)
        z = _dot(u, win_ref[...])

        gla_ref[rows, GC_Q:GC_K] = z[:, ZC_Q:ZC_K] * (GLA_DK ** -0.5)
        gla_ref[rows, GC_K:GC_G] = z[:, ZC_K:ZC_CQ]
        gx = _dot(z[:, ZC_GA:ZC_END].astype(BF16), wgate_ref[...]) + bgate_ref[...]
        log_sig = jnp.minimum(gx, 0.0) - jnp.log(1.0 + jnp.exp(-jnp.abs(gx)))
        gla_ref[rows, GC_G:GC_END] = log_sig / GLA_TAU

        ckvn = _rms(z[:, ZC_CKV:ZC_KRD], gckv_ref[...])
        ckvn_ref[rows, :] = ckvn
        cb = ckvn.astype(BF16)
        vbt_ref[0, :, rows] = ckvn.T.astype(BF16)
        krd = z[:, ZC_KRD:ZC_GA]
        kr_ref[rows, :] = krd[:, 0:QK_ROPE]
        kr_ss = 0.5 * jnp.sum(krd * krd, axis=-1, keepdims=True)
        kn = _dot(cb, wuk_ref[...])

        cqn = _rms(z[:, ZC_CQ:ZC_CKV], gcq_ref[...]).astype(BF16)
        qq = _dot(cqn, wuq_ref[...])

        tab = t_ref[rows, :]
        gqn, gqd, gkn, gkd = gqn_ref[...], gqd_ref[...], gkn_ref[...], gkd_ref[...]
        low = lax.broadcasted_iota(jnp.int32, (1, LANES), 1) < QK_ROPE
        nh = MLA_HEADS * QK_NOPE
        for hh in range(MLA_HEADS):
            lo, hi = hh * QK_NOPE, (hh + 1) * QK_NOPE
            qn = qq[:, lo:hi]
            qd = qq[:, nh + lo:nh + hi]
            ss = jnp.sum(qn * qn, axis=-1, keepdims=True) + 0.5 * jnp.sum(qd * qd, axis=-1, keepdims=True)
            rq = lax.rsqrt(ss / QK_HEAD + EPS) * (QK_SCALE * LOG2E)
            qnope = qn * rq * gqn
            aq = qd * rq * gqd * tab
            qrope = jnp.where(low, aq + pltpu.roll(aq, QK_ROPE, axis=1), 0.0)
            qh_ref[rows, hh * HEAD_PAD:hh * HEAD_PAD + QK_NOPE] = qnope.astype(BF16)
            qh_ref[rows, hh * HEAD_PAD + QK_NOPE:(hh + 1) * HEAD_PAD] = qrope.astype(BF16)
            qabs = _dot_nt((qnope * gkn).astype(BF16), wuk_ref[:, lo:hi])
            qabs_ref[rows, lo:hi] = qabs.astype(BF16)

            knh = kn[:, lo:hi]
            rk = lax.rsqrt((jnp.sum(knh * knh, axis=-1, keepdims=True) + kr_ss) / QK_HEAD + EPS)
            kh_ref[rows, hh * HEAD_PAD:hh * HEAD_PAD + QK_NOPE] = (knh * rk * gkn).astype(BF16)
            ak = krd * rk * gkd * tab
            krope = jnp.where(low, ak + pltpu.roll(ak, QK_ROPE, axis=1), 0.0)
            kh_ref[rows, hh * HEAD_PAD + QK_NOPE:(hh + 1) * HEAD_PAD] = krope.astype(BF16)


def _mixprep_call(h, gmix, win, wgate, bgate, gcq, wuq, gqn, gqd, gckv, wuk, gkn, gkd, tab, tm, layer):
    n, d = h.shape
    row = lambda w: pl.BlockSpec((tm, w), lambda i: (i, 0))
    consts = [gmix, win, wgate, bgate, gcq, wuq, gqn, gqd, gckv, wuk, gkn, gkd]
    hq = MLA_HEADS * HEAD_PAD
    out_shape = [
        jax.ShapeDtypeStruct((n, GC_END), F32),
        jax.ShapeDtypeStruct((n, KV_LORA), F32),
        jax.ShapeDtypeStruct((n, QK_ROPE), F32),
        jax.ShapeDtypeStruct((n, hq), BF16),
        jax.ShapeDtypeStruct((n, hq), BF16),
        jax.ShapeDtypeStruct((n // tm, KV_LORA, tm), BF16),
        jax.ShapeDtypeStruct((n, MLA_HEADS * KV_LORA), BF16),
    ]
    out_specs = [row(s.shape[1]) for s in out_shape]
    out_specs[5] = pl.BlockSpec((1, KV_LORA, tm), lambda i: (i, 0, 0))
    return pl.pallas_call(
        functools.partial(_mixprep_kernel, parts=2 if tm % (2 * LANES) == 0 else 1),
        grid=(n // tm,),
        in_specs=[row(d)] + [_layer_spec(c, layer) for c in consts] + [row(LANES)],
        out_specs=out_specs,
        out_shape=out_shape,
        compiler_params=pltpu.CompilerParams(
            dimension_semantics=("arbitrary",), vmem_limit_bytes=VMEM_LIMIT),
        name="mixprep",
    )(h, *consts, tab)


def _gla_out(o, ra, gout):
    outs = []
    for hh in range(GLA_HEADS):
        oh = o[:, hh * GLA_DV:(hh + 1) * GLA_DV]
        outs.append(_rms(oh, gout) * _silu(ra[:, hh * GLA_DV:(hh + 1) * GLA_DV]))
    return outs


def _gla_prompt_kernel(*refs, tc, nb):
    gin_refs, (gout_ref, o_ref, st_ref, s_ref) = refs[:nb], refs[nb:]
    i = pl.program_id(0)
    c_len = GLA_CHUNK
    kd = GLA_HEADS * GLA_DK
    vd = GLA_HEADS * GLA_DV

    @pl.when(i == 0)
    def _():
        s_ref[...] = jnp.zeros_like(s_ref)

    ri = lax.broadcasted_iota(jnp.int32, (c_len, c_len), 0)
    ci = lax.broadcasted_iota(jnp.int32, (c_len, c_len), 1)
    tri = (ci <= ri).astype(BF16)
    k_head = lax.broadcasted_iota(jnp.int32, (1, kd), 1) // GLA_DK
    v_head = lax.broadcasted_iota(jnp.int32, (1, vd), 1) // GLA_DV
    wide_r = lax.broadcasted_iota(jnp.int32, (c_len, kd), 0)
    wide_s = lax.broadcasted_iota(jnp.int32, (c_len, kd), 1) % c_len
    causal_wide = wide_s <= wide_r
    bd = (lax.broadcasted_iota(jnp.int32, (kd, vd), 0) // GLA_DK
          == lax.broadcasted_iota(jnp.int32, (kd, vd), 1) // GLA_DV)
    gout = gout_ref[...]

    for c, bb in ((c, bb) for c in range(tc // c_len) for bb in range(nb)):
        gin_ref = gin_refs[bb]
        rows = slice(c * c_len, (c + 1) * c_len)
        q = gin_ref[rows, GC_Q:GC_K]
        k = gin_ref[rows, GC_K:GC_V]
        v = gin_ref[rows, GC_V:GC_R]
        ra = gin_ref[rows, GC_R:GC_G]
        g = gin_ref[rows, GC_G:GC_END]
        b = _dot3(tri, _split3(g))
        b_last = b[c_len - 1:c_len, :]
        q_t = (q * jnp.exp(b)).astype(BF16)
        k_t = k * jnp.exp(-b)
        k_dec = k * jnp.exp(b_last - b)
        vb = v.astype(BF16)

        kstack = jnp.concatenate(
            [jnp.where(k_head == hh, k_t, 0.0) for hh in range(GLA_HEADS)], axis=0).astype(BF16)
        a = jnp.where(causal_wide, _dot_nt(q_t, kstack), 0.0)
        vbd = jnp.concatenate(
            [jnp.where(v_head == hh, v, 0.0) for hh in range(GLA_HEADS)], axis=0).astype(BF16)
        s_prev = s_ref[bb]
        o = _dot(a.astype(BF16), vbd) + _dot(q_t, s_prev.astype(BF16))

        kpad = jnp.concatenate(
            [k_dec, jnp.broadcast_to(b_last, (SUBLANES, kd)),
             jnp.zeros((LANES - c_len - SUBLANES, kd), F32)], axis=0)
        kdt = kpad.T
        decay_col = jnp.exp(kdt[:, c_len:c_len + 1])
        vpad = jnp.concatenate([vb, jnp.zeros((LANES - c_len, vd), BF16)], axis=0)
        ds = _dot(kdt.astype(BF16), vpad)
        s_ref[bb] = decay_col * s_prev + jnp.where(bd, ds, 0.0)

        outs = _gla_out(o, ra, gout)
        for hh in range(GLA_HEADS):
            o_ref[bb, rows, hh * GLA_DV:(hh + 1) * GLA_DV] = outs[hh].astype(BF16)

    @pl.when(i == pl.num_programs(0) - 1)
    def _():
        for bb in range(nb):
            for hh in range(GLA_HEADS):
                st_ref[bb, hh] = s_ref[bb, hh * GLA_DK:(hh + 1) * GLA_DK, hh * GLA_DV:(hh + 1) * GLA_DV]


def _gla_prompt_call(gla, gout, batch, seq, layer):
    assert batch <= SUBLANES
    tc = _pick_tile(seq, (256, 128, 64))
    nc = seq // tc
    vd = GLA_HEADS * GLA_DV
    seq_spec = lambda bb: pl.BlockSpec((tc, GC_END), lambda i: (bb * nc + i, 0))
    o, st = pl.pallas_call(
        functools.partial(_gla_prompt_kernel, tc=tc, nb=batch),
        grid=(nc,),
        in_specs=[seq_spec(bb) for bb in range(batch)] + [_layer_spec(gout, layer)],
        out_specs=[
            pl.BlockSpec((batch, tc, vd), lambda i: (0, i, 0)),
            pl.BlockSpec((batch, GLA_HEADS, GLA_DK, GLA_DV), lambda i: (0, 0, 0, 0)),
        ],
        out_shape=[
            jax.ShapeDtypeStruct((batch, seq, vd), BF16),
            jax.ShapeDtypeStruct((batch, GLA_HEADS, GLA_DK, GLA_DV), F32),
        ],
        scratch_shapes=[pltpu.VMEM((batch, GLA_HEADS * GLA_DK, vd), F32)],
        compiler_params=pltpu.CompilerParams(
            dimension_semantics=("arbitrary",), vmem_limit_bytes=VMEM_LIMIT),
        name="gla_prompt",
    )(*([gla] * batch), gout)
    return o.reshape(batch * seq, vd), st


def _gla_sample_kernel(gin_ref, s0_ref, gout_ref, o_ref, sn_ref, *, nreq):
    rows = nreq * T_PAD
    kd = GLA_HEADS * GLA_DK
    q = gin_ref[:, GC_Q:GC_K]
    k = gin_ref[:, GC_K:GC_V]
    v = gin_ref[:, GC_V:GC_R]
    ra = gin_ref[:, GC_R:GC_G]
    g = gin_ref[:, GC_G:GC_END]
    ri = lax.broadcasted_iota(jnp.int32, (rows, rows), 0)
    ci = lax.broadcasted_iota(jnp.int32, (rows, rows), 1)
    same = (ri // T_PAD) == (ci // T_PAD)
    causal = same & (ci <= ri)
    g3 = _split3(g)
    b = _dot3(causal.astype(BF16), g3)
    b_last = _dot3(same.astype(BF16), g3)
    q_t = q * jnp.exp(b)
    k_t = (k * jnp.exp(-b)).astype(BF16)
    k_dec = k * jnp.exp(b_last - b)
    vb = v.astype(BF16)
    k_head = lax.broadcasted_iota(jnp.int32, (1, kd), 1) // GLA_DK

    o_intra = []
    for hh in range(GLA_HEADS):
        a = _dot_nt(jnp.where(k_head == hh, q_t, 0.0).astype(BF16), k_t)
        a = jnp.where(causal, a, 0.0).astype(BF16)
        o_intra.append(_dot(a, vb[:, hh * GLA_DV:(hh + 1) * GLA_DV]))

    o_inter = []
    for r in range(nreq):
        qr = q_t[r * T_PAD:(r + 1) * T_PAD, :]
        qm = jnp.concatenate(
            [jnp.where(k_head == hh, qr, 0.0) for hh in range(GLA_HEADS)], axis=0).astype(BF16)
        o_inter.append(_dot(qm, s0_ref[r * kd:(r + 1) * kd, :].astype(BF16)))

    gout = gout_ref[...]
    for hh in range(GLA_HEADS):
        inter = jnp.concatenate(
            [o_inter[r][hh * T_PAD:(hh + 1) * T_PAD, :] for r in range(nreq)], axis=0)
        oh = o_intra[hh] + inter
        y = _rms(oh, gout) * _silu(ra[:, hh * GLA_DV:(hh + 1) * GLA_DV])
        o_ref[:, hh * GLA_DV:(hh + 1) * GLA_DV] = y.astype(BF16)

    kdt = k_dec.T
    blt = b_last.T
    lane = lax.broadcasted_iota(jnp.int32, (1, rows), 1)
    for hh in range(GLA_HEADS):
        kh = kdt[hh * GLA_DK:(hh + 1) * GLA_DK, :]
        bh = blt[hh * GLA_DK:(hh + 1) * GLA_DK, :]
        lhs = jnp.concatenate(
            [jnp.where(lane // T_PAD == r, kh, 0.0) for r in range(nreq)], axis=0).astype(BF16)
        ds = _dot(lhs, vb[:, hh * GLA_DV:(hh + 1) * GLA_DV])
        for r in range(nreq):
            dec = jnp.exp(jnp.sum(jnp.where(lane == r * T_PAD, bh, 0.0), axis=-1, keepdims=True))
            lo = r * kd + hh * GLA_DK
            sn_ref[lo:lo + GLA_DK, :] = dec * s0_ref[lo:lo + GLA_DK, :] + ds[r * GLA_DK:(r + 1) * GLA_DK, :]


def _gla_sample_call(gla_pad, s0, gout, dec_batch, layer):
    nreq = _pick_tile(dec_batch, (16, 8, 4, 2, 1))
    rows = nreq * T_PAD
    kd = GLA_HEADS * GLA_DK
    vd = GLA_HEADS * GLA_DV
    return pl.pallas_call(
        functools.partial(_gla_sample_kernel, nreq=nreq),
        grid=(dec_batch // nreq,),
        in_specs=[
            pl.BlockSpec((rows, GC_END), lambda i: (i, 0)),
            pl.BlockSpec((None, nreq * kd, GLA_DV), lambda i: (layer, i, 0)),
            _layer_spec(gout, layer),
        ],
        out_specs=[
            pl.BlockSpec((rows, vd), lambda i: (i, 0)),
            pl.BlockSpec((nreq * kd, GLA_DV), lambda i: (i, 0)),
        ],
        out_shape=[
            jax.ShapeDtypeStruct((dec_batch * T_PAD, vd), BF16),
            jax.ShapeDtypeStruct((dec_batch * kd, GLA_DV), F32),
        ],
        compiler_params=pltpu.CompilerParams(
            dimension_semantics=("arbitrary",), vmem_limit_bytes=VMEM_LIMIT),
        name="gla_sample",
    )(gla_pad, s0, gout)


def _mla_prompt_kernel(q_ref, k_ref, vt_ref, wuvt_ref, o_ref, qt_ref, m_ref, l_ref, acc_ref, *, tq):
    i = pl.program_id(1)
    qt_ref[...] = q_ref[...].astype(F32).T.astype(BF16)
    m_ref[...] = jnp.full(m_ref.shape, -jnp.inf, F32)
    l_ref[...] = jnp.zeros_like(l_ref)
    acc_ref[...] = jnp.zeros_like(acc_ref)
    key_i = lax.broadcasted_iota(jnp.int32, (tq, tq), 0)
    qry_i = lax.broadcasted_iota(jnp.int32, (tq, tq), 1)
    diag_ok = key_i <= qry_i

    def block(j, masked):
        k0 = pl.multiple_of(j * tq, tq)
        kblk = k_ref[pl.ds(k0, tq), :]
        ps, alphas = [], []
        for hh in range(MLA_HEADS):
            st = _dot(kblk[:, hh * HEAD_PAD:(hh + 1) * HEAD_PAD], qt_ref[hh * HEAD_PAD:(hh + 1) * HEAD_PAD, :])
            if masked:
                st = jnp.where(diag_ok, st, -jnp.inf)
            m_prev = m_ref[hh:hh + 1, :]
            m_new = jnp.maximum(m_prev, jnp.max(st, axis=0, keepdims=True))
            alpha = jnp.exp2(m_prev - m_new)
            p = jnp.exp2(st - m_new)
            l_ref[hh:hh + 1, :] = alpha * l_ref[hh:hh + 1, :] + jnp.sum(p, axis=0, keepdims=True)
            m_ref[hh:hh + 1, :] = m_new
            ps.append(p.astype(BF16))
            alphas.append(alpha)
        pv = _dot(vt_ref[j], jnp.concatenate(ps, axis=1))
        acc_ref[...] = acc_ref[...] * jnp.concatenate(alphas, axis=1) + pv

    def body(j, carry):
        block(j, False)
        return carry

    lax.fori_loop(0, i, body, 0)
    block(i, True)
    inv_l = 1.0 / jnp.concatenate([l_ref[hh:hh + 1, :] for hh in range(MLA_HEADS)], axis=1)
    lat_t = (acc_ref[...] * inv_l).astype(BF16)
    out_t = jnp.concatenate(
        [_dot(wuvt_ref[hh * V_HEAD:(hh + 1) * V_HEAD, :], lat_t[:, hh * tq:(hh + 1) * tq])
         for hh in range(MLA_HEADS)], axis=0)
    o_ref[...] = out_t.T.astype(BF16)


def _mla_prompt_call(qh, kh, vbt, wuvt, batch, seq, tq, layer):
    nq = seq // tq
    hq = MLA_HEADS * HEAD_PAD
    od = MLA_HEADS * V_HEAD
    return pl.pallas_call(
        functools.partial(_mla_prompt_kernel, tq=tq),
        grid=(batch, nq),
        in_specs=[
            pl.BlockSpec((tq, hq), lambda b, i: (b * nq + i, 0)),
            pl.BlockSpec((seq, hq), lambda b, i: (b, 0)),
            pl.BlockSpec((nq, KV_LORA, tq), lambda b, i: (b, 0, 0)),
            _layer_spec(wuvt, layer),
        ],
        out_specs=pl.BlockSpec((tq, od), lambda b, i: (b * nq + i, 0)),
        out_shape=jax.ShapeDtypeStruct((batch * seq, od), BF16),
        scratch_shapes=[
            pltpu.VMEM((hq, tq), BF16),
            pltpu.VMEM((SUBLANES, tq), F32),
            pltpu.VMEM((SUBLANES, tq), F32),
            pltpu.VMEM((KV_LORA, MLA_HEADS * tq), F32),
        ],
        compiler_params=pltpu.CompilerParams(
            dimension_semantics=("arbitrary", "arbitrary"), vmem_limit_bytes=VMEM_LIMIT),
        name="mla_prompt",
    )(qh, kh, vbt, wuvt)


def _mla_sample_kernel(pt_ref, qabs_ref, qpe_ref, cnew_ref, krnew_ref, wukt_ref, gpe_ref, cos_ref, sin_ref,
                       wuv_ref, ckv_hbm, krt_hbm, o_ref, cbuf, kbuf, cbf, sc_ref, rt_ref, sem,
                       *, layer, n_pages, page, kc, tw):
    r = pl.program_id(0)
    nreq = pl.num_programs(0)
    slot = r % 2
    past = n_pages * page
    tail = LANES
    ppc = kc // page
    n_chunks = past // kc

    def page_copies(req, sl, c, i):
        p = c * ppc + i
        pg = pt_ref[req * n_pages + p]
        dst = pl.ds(p * page, page) if isinstance(p, int) else pl.ds(pl.multiple_of(p * page, page), page)
        return (pltpu.make_async_copy(ckv_hbm.at[layer, pg], cbuf.at[sl, dst], sem.at[sl, 0, c]),
                pltpu.make_async_copy(krt_hbm.at[layer, pg], kbuf.at[sl, p], sem.at[sl, 1, c]))

    half = QK_ROPE // 2
    nxt = jnp.minimum(r + 1, nreq - 1)

    @pl.when(r == 0)
    def _():
        for sl in range(2):
            cbuf[sl, past:past + tail, :] = jnp.zeros((tail, KV_LORA), F32)

        def first(p, carry):
            for c in range(n_chunks):
                for cp in page_copies(0, 0, c, p):
                    cp.start()
            return carry
        lax.fori_loop(0, ppc, first, 0)
        g1, g2 = gpe_ref[0:half, :], gpe_ref[half:QK_ROPE, :]
        rt_ref[0] = cos_ref[...] * g1
        rt_ref[1] = sin_ref[...] * g2
        rt_ref[2] = sin_ref[...] * g1
        rt_ref[3] = cos_ref[...] * g2

    cbuf[slot, past:past + T_PAD, :] = cnew_ref[0]
    kbuf[slot, n_pages] = krnew_ref[0]

    nrow = MLA_HEADS * T_PAD
    lhs = jnp.concatenate([wukt_ref[...], qabs_ref[0]], axis=0)
    qpe = qpe_ref[0]
    nk = MLA_HEADS * QK_NOPE

    def features(k0, width):
        cb = cbuf[slot, k0:k0 + width, :].astype(BF16)
        cbf[k0:k0 + width, :] = cb
        return _dot_nt(lhs, cb)

    def scores(res_rows, krt, k0, width, is_tail):
        ss_kr = _sum_rows(krt * krt)
        x1, x2 = krt[0:half], krt[half:QK_ROPE]
        cols = slice(k0, k0 + width)
        rot = jnp.concatenate([x1 * rt_ref[0, :, cols] - x2 * rt_ref[1, :, cols],
                               x1 * rt_ref[2, :, cols] + x2 * rt_ref[3, :, cols]], axis=0).astype(BF16)
        s = res_rows(nk, nk + nrow) + _dot(qpe, rot)
        rks = []
        for hh in range(MLA_HEADS):
            knh = res_rows(hh * QK_NOPE, (hh + 1) * QK_NOPE)
            rk = lax.rsqrt((_sum_rows(knh * knh) + ss_kr) / QK_HEAD + EPS)
            rks.append(jnp.broadcast_to(rk, (T_PAD, width)))
        s = s * jnp.concatenate(rks, axis=0)
        if is_tail:
            tok = lax.broadcasted_iota(jnp.int32, (nrow, width), 0) % T_PAD
            key = lax.broadcasted_iota(jnp.int32, (nrow, width), 1)
            s = jnp.where(key <= tok, s, -jnp.inf)
        sc_ref[:, cols] = s

    ppt = tw // page
    for c in range(n_chunks):
        for i in range(ppc):
            for cp in page_copies(nxt, 1 - slot, c, i):
                cp.start()
        for i in range(ppc):
            for cp in page_copies(r, slot, c, i):
                cp.wait()
        for t in range(kc // tw):
            k0 = c * kc + t * tw
            res = features(k0, tw)
            krt = jnp.concatenate([kbuf[slot, k0 // page + i] for i in range(ppt)], axis=1)
            scores(lambda lo, hi: res[lo:hi], krt, k0, tw, False)
    res_tail = features(past, tail)
    scores(lambda lo, hi: res_tail[lo:hi], kbuf[slot, n_pages], past, tail, True)

    s_all = sc_ref[...]
    m = jnp.max(s_all, axis=-1, keepdims=True)
    p = jnp.exp2(s_all - m)
    l = jnp.sum(p, axis=-1, keepdims=True)
    lat = _dot(p.astype(BF16), cbf[...]) / l
    for hh in range(MLA_HEADS):
        lh = lat[hh * T_PAD:(hh + 1) * T_PAD, :].astype(BF16)
        o_ref[0, hh * T_PAD:(hh + 1) * T_PAD, :] = _dot(lh, wuv_ref[:, hh * V_HEAD:(hh + 1) * V_HEAD]).astype(BF16)

    @pl.when(r == nreq - 1)
    def _():
        def drain(i, carry):
            for c in range(n_chunks):
                for cp in page_copies(nxt, 1 - slot, c, i):
                    cp.wait()
            return carry
        lax.fori_loop(0, ppc, drain, 0)


def _mla_sample_call(page_table, qabs, qpe, cnew, krnew_t, wukt, gpe, cos_t, sin_t, wuv, cache_ckv, cache_krope_t,
                     layer):
    dec_batch, n_pages = page_table.shape
    page = cache_ckv.shape[2]
    assert page == LANES and cache_krope_t.shape[2:] == (QK_ROPE, page)
    past = n_pages * page
    kc = _pick_tile(past // 2, (4096, 2048, 1024, 512, 256, 128))
    nrow = MLA_HEADS * T_PAD
    total = past + LANES
    req = lambda w: pl.BlockSpec((1, nrow, w), lambda r, pt: (r, 0, 0))
    const = lambda shape: pl.BlockSpec(shape, lambda r, pt: (0,) * len(shape), pipeline_mode=pl.Buffered(1))
    grid_spec = pltpu.PrefetchScalarGridSpec(
        num_scalar_prefetch=1,
        grid=(dec_batch,),
        in_specs=[
            req(KV_LORA), req(QK_ROPE),
            pl.BlockSpec((1, T_PAD, KV_LORA), lambda r, pt: (r, 0, 0)),
            pl.BlockSpec((1, QK_ROPE, LANES), lambda r, pt: (r, 0, 0)),
            _layer_spec(wukt, layer), _layer_spec(gpe, layer), const(cos_t.shape), const(sin_t.shape),
            _layer_spec(wuv, layer),
            pl.BlockSpec(memory_space=pl.ANY), pl.BlockSpec(memory_space=pl.ANY),
        ],
        out_specs=pl.BlockSpec((1, nrow, V_HEAD), lambda r, pt: (r, 0, 0)),
        scratch_shapes=[
            pltpu.VMEM((2, total, KV_LORA), F32),
            pltpu.VMEM((2, n_pages + 1, QK_ROPE, page), F32),
            pltpu.VMEM((total, KV_LORA), BF16),
            pltpu.VMEM((nrow, total), F32),
            pltpu.VMEM((4, QK_ROPE // 2, total), F32),
            pltpu.SemaphoreType.DMA((2, 2, past // kc)),
        ],
    )
    return pl.pallas_call(
        functools.partial(_mla_sample_kernel, layer=layer, n_pages=n_pages, page=page, kc=kc, tw=kc),
        grid_spec=grid_spec,
        out_shape=jax.ShapeDtypeStruct((dec_batch, nrow, V_HEAD), BF16),
        compiler_params=pltpu.CompilerParams(
            dimension_semantics=("arbitrary",), vmem_limit_bytes=VMEM_LIMIT),
        name="mla_sample",
    )(page_table.reshape(-1), qabs, qpe, cnew, krnew_t, wukt, gpe, cos_t, sin_t, wuv, cache_ckv, cache_krope_t)


def _mid_kernel(h_ref, oap_ref, obp_ref, pp_ref, oas_ref, obs_ref, ps_ref, wa_ref, wb_ref, g2_ref, wgu_ref, wd_ref,
                gp_ref, wpg_ref, wp_ref, *rest, ffn, ch, prompt_tiles, split_out):
    outs, (x_ref, acc_ref, pe_ref) = rest[:-3], rest[-3:]
    i = pl.program_id(0)

    def load(oa_ref, ob_ref, p_ref):
        x_ref[...] = h_ref[...] + _dot(oa_ref[...], wa_ref[...]) + _dot(ob_ref[...], wb_ref[...])
        pe_ref[...] = p_ref[...].astype(BF16)

    @pl.when(i < prompt_tiles)
    def _():
        load(oap_ref, obp_ref, pp_ref)

    @pl.when(i >= prompt_tiles)
    def _():
        load(oas_ref, obs_ref, ps_ref)

    x = x_ref[...]
    _swiglu_into(acc_ref, _rms(x, g2_ref[...]).astype(BF16), wgu_ref, wd_ref, ffn, ch)
    x = x + 0.5 * acc_ref[...]
    gate = jax.nn.sigmoid(_dot(_rms(x, gp_ref[...]).astype(BF16), wpg_ref[...]))
    y = x + gate * _dot(pe_ref[...], wp_ref[...])
    if split_out:
        @pl.when(i < prompt_tiles)
        def _():
            outs[0][...] = y

        @pl.when(i >= prompt_tiles)
        def _():
            outs[1][...] = y
    else:
        outs[0][...] = y


def _mid_call(h, oa_p, ob_p, p_p, oa_s, ob_s, p_s, wa, wb, g2, wgu, wd, gp, wpg, wp, layer, split_out):
    n, d = h.shape
    ffn = wd.shape[1]
    np_rows, ns_rows = oa_p.shape[0], oa_s.shape[0]
    tm = _pick_tile(math.gcd(np_rows, ns_rows), (512, 256, 128))
    assert np_rows + ns_rows == n
    pt = np_rows // tm
    ch = _pick_tile(ffn, (256, 128))
    row = pl.BlockSpec((tm, d), lambda i: (i, 0))
    oa_specs = _split_rows(tm, oa_p.shape[1], pt)
    ob_specs = _split_rows(tm, ob_p.shape[1], pt)
    pw = p_p.shape[2]
    pp_spec = pl.BlockSpec((None, tm, pw), lambda i: (layer, jnp.minimum(i, pt - 1), 0))
    ps_spec = pl.BlockSpec((None, tm, pw), lambda i: (layer, jnp.maximum(i - pt, 0), 0))
    consts = [wa, wb, g2, wgu, wd, gp, wpg, wp]
    if split_out:
        out_specs = list(_split_rows(tm, d, pt))
        out_shape = [jax.ShapeDtypeStruct((np_rows, d), F32), jax.ShapeDtypeStruct((ns_rows, d), F32)]
    else:
        out_specs, out_shape = row, jax.ShapeDtypeStruct((n, d), F32)
    return pl.pallas_call(
        functools.partial(_mid_kernel, ffn=ffn, ch=ch, prompt_tiles=pt, split_out=split_out),
        grid=(n // tm,),
        in_specs=([row, oa_specs[0], ob_specs[0], pp_spec, oa_specs[1], ob_specs[1], ps_spec]
                  + [_layer_spec(c, layer) for c in consts]),
        out_specs=out_specs,
        out_shape=out_shape,
        scratch_shapes=[pltpu.VMEM((tm, d), F32), pltpu.VMEM((tm, d), F32), pltpu.VMEM((tm, pw), BF16)],
        compiler_params=pltpu.CompilerParams(
            dimension_semantics=("arbitrary",), vmem_limit_bytes=VMEM_LIMIT),
        name="mid",
    )(h, oa_p, ob_p, p_p, oa_s, ob_s, p_s, *consts)


def _rope_tables(pos):
    inv = ROPE_THETA ** (-jnp.arange(0, QK_ROPE, 2, dtype=F32) / QK_ROPE)
    ang = pos.astype(F32)[:, None] * inv[None, :]
    return jnp.cos(ang), jnp.sin(ang)


def _dup_rope(x):
    x1, x2 = x[..., :QK_ROPE // 2], x[..., QK_ROPE // 2:]
    return jnp.concatenate([x1, x2, x2, x1], axis=-1)


def kernel(x_prompt, x_sample, cache_ckv, cache_krope, state_gla, page_table, p_prompt, p_sample, g_ffn1, w_ffn1_gu, w_ffn1_d, g_mix, w_in, w_gla_gate, b_gla_gate, g_gla_out, g_cq, w_uq, g_ckv, w_uk, w_uv, g_q, g_k, w_out, g_ffn2, w_ffn2_gu, w_ffn2_d, g_ple, w_ple_gate, w_ple):
    batch, seq, d = x_prompt.shape
    dec_batch, dec_seq, _ = x_sample.shape
    depth = w_in.shape[0]
    n_pages, page = page_table.shape[1], cache_ckv.shape[2]
    past = n_pages * page
    np_rows = batch * seq
    ns_rows = dec_batch * dec_seq
    assert dec_seq <= T_PAD

    kd = GLA_HEADS * GLA_DK
    vd = GLA_HEADS * GLA_DV
    o = 0
    cols = {}
    for name, w in (("q", kd), ("k", kd), ("v", vd), ("r", vd), ("ga", GLA_GATE_RANK), ("cq", Q_LORA),
                    ("ckv", KV_LORA), ("kr", QK_ROPE)):
        cols[name] = w_in[:, :, o:o + w]
        o += w
    win_p = jnp.concatenate(
        [cols["q"], cols["k"], cols["v"], cols["r"], cols["cq"], cols["ckv"], _dup_rope(cols["kr"]), cols["ga"],
         jnp.zeros((depth, d, LANES - GLA_GATE_RANK), F32)], axis=-1).astype(BF16)
    wgate_p = jnp.concatenate(
        [w_gla_gate, jnp.zeros((depth, LANES - GLA_GATE_RANK, kd), F32)], axis=1).astype(BF16)
    wuq4 = w_uq.reshape(depth, Q_LORA, MLA_HEADS, QK_HEAD)
    wuq_p = jnp.concatenate(
        [wuq4[..., :QK_NOPE].reshape(depth, Q_LORA, -1), _dup_rope(wuq4[..., QK_NOPE:]).reshape(depth, Q_LORA, -1)],
        axis=-1).astype(BF16)
    wuk_b = w_uk.astype(BF16)
    wukt_b = jnp.swapaxes(w_uk, 1, 2).astype(BF16)
    wuv_b = w_uv.astype(BF16)
    wuvt_b = jnp.swapaxes(w_uv, 1, 2).astype(BF16)
    wout_a, wout_b = w_out[:, :vd].astype(BF16), w_out[:, vd:].astype(BF16)
    wf1gu, wf1d = w_ffn1_gu.astype(BF16), w_ffn1_d.astype(BF16)
    wf2gu, wf2d = w_ffn2_gu.astype(BF16), w_ffn2_d.astype(BF16)
    wpg_b, wp_b = w_ple_gate.astype(BF16), w_ple.astype(BF16)
    row2 = lambda a: a.reshape(depth, 1, -1)
    gqn, gqd = row2(g_q[:, :QK_NOPE]), row2(_dup_rope(g_q[:, QK_NOPE:]))
    gkn, gkd = row2(g_k[:, :QK_NOPE]), row2(_dup_rope(g_k[:, QK_NOPE:]))
    gpe_col = g_k[:, QK_NOPE:].reshape(depth, QK_ROPE, 1)
    gf1, gmix, bgate, gcq, gckv, gout, gf2, gple = (
        row2(a) for a in (g_ffn1, g_mix, b_gla_gate, g_cq, g_ckv, g_gla_out, g_ffn2, g_ple))

    def rope_rows(pos, reps):
        cs, sn = _rope_tables(pos)
        return jnp.tile(jnp.concatenate([cs, cs, -sn, sn], axis=-1), (reps, 1))

    tab = jnp.concatenate([rope_rows(jnp.arange(seq), batch), rope_rows(past + jnp.arange(dec_seq), dec_batch)])
    cs_k, sn_k = _rope_tables(jnp.arange(past + LANES))
    cos_t, sin_t = cs_k.T, sn_k.T

    x = (x_prompt.reshape(np_rows, d), x_sample.reshape(ns_rows, d))
    pp_all = p_prompt.reshape(depth, np_rows, -1)
    ps_all = p_sample.reshape(depth, ns_rows, -1)
    s0_all = state_gla.reshape(depth, dec_batch * kd, GLA_DV)

    pad_t = lambda a: jnp.pad(a.reshape(dec_batch, dec_seq, -1), ((0, 0), (0, T_PAD - dec_seq), (0, 0)))
    unpad = lambda a: a.reshape(dec_batch, T_PAD, -1)[:, :dec_seq].reshape(ns_rows, -1)

    def heads_rows(a, width):
        a = a.reshape(dec_batch, dec_seq, MLA_HEADS, width).transpose(0, 2, 1, 3)
        a = jnp.pad(a, ((0, 0), (0, 0), (0, T_PAD - dec_seq), (0, 0)))
        return a.reshape(dec_batch, MLA_HEADS * T_PAD, width)

    cache_krope_t = jnp.swapaxes(cache_krope, 2, 3)
    tile = _pick_tile(math.gcd(seq, ns_rows), (512, 256, 128))

    outs = [[] for _ in range(6)]
    for l in range(depth):
        h1 = _ffn_call(x, gf1, wf1gu, wf1d, l)
        gla, ckvn, kr, qh, kh, vbt, qabs = _mixprep_call(
            h1, gmix, win_p, wgate_p, bgate, gcq, wuq_p, gqn, gqd, gckv, wuk_b, gkn, gkd, tab, tile, l)

        oa_p, st_p = _gla_prompt_call(gla, gout, batch, seq, l)
        oa_s, st_s = _gla_sample_call(pad_t(gla[np_rows:]).reshape(dec_batch * T_PAD, -1), s0_all, gout, dec_batch, l)
        ob_p = _mla_prompt_call(qh, kh, vbt, wuvt_b, batch, seq, tile, l)
        qh_s = qh[np_rows:].reshape(ns_rows, MLA_HEADS, HEAD_PAD)[:, :, QK_NOPE:QK_NOPE + QK_ROPE]
        krnew_t = jnp.pad(jnp.swapaxes(kr[np_rows:].reshape(dec_batch, dec_seq, QK_ROPE), 1, 2),
                          ((0, 0), (0, 0), (0, LANES - dec_seq)))
        ob_s = _mla_sample_call(
            page_table, heads_rows(qabs[np_rows:], KV_LORA), heads_rows(qh_s, QK_ROPE), pad_t(ckvn[np_rows:]),
            krnew_t, wukt_b, gpe_col, cos_t, sin_t, wuv_b, cache_ckv, cache_krope_t, l)
        ob_s = ob_s.reshape(dec_batch, MLA_HEADS, T_PAD, V_HEAD)[:, :, :dec_seq].transpose(0, 2, 1, 3)

        x = _mid_call(h1, oa_p, ob_p, pp_all, unpad(oa_s), ob_s.reshape(ns_rows, -1), ps_all, wout_a, wout_b,
                      gf2, wf2gu, wf2d, gple, wpg_b, wp_b, l, split_out=(l == depth - 1))

        outs[0].append(ckvn[:np_rows].reshape(batch, seq, KV_LORA))
        outs[1].append(kr[:np_rows].reshape(batch, seq, QK_ROPE))
        outs[2].append(st_p)
        outs[3].append(ckvn[np_rows:].reshape(dec_batch, dec_seq, KV_LORA))
        outs[4].append(kr[np_rows:].reshape(dec_batch, dec_seq, QK_ROPE))
        outs[5].append(st_s.reshape(dec_batch, GLA_HEADS, GLA_DK, GLA_DV))

    y_prompt, y_sample = x
    return (y_prompt.reshape(batch, seq, d), y_sample.reshape(dec_batch, dec_seq, d),
            *(jnp.stack(o) for o in outs))
```

```python
import functools
import math

import jax
import jax.numpy as jnp
from jax import lax
from jax.experimental import pallas as pl
from jax.experimental.pallas import tpu as pltpu

F32 = jnp.float32
BF16 = jnp.bfloat16

EPS = 1e-6
GLA_HEADS = 4
GLA_DK = 64
GLA_DV = 128
GLA_GATE_RANK = 16
GLA_TAU = 16.0
GLA_CHUNK = 64
MLA_HEADS = 4
Q_LORA = 256
KV_LORA = 128
QK_NOPE = 128
QK_ROPE = 64
QK_HEAD = QK_NOPE + QK_ROPE
V_HEAD = 128
ROPE_THETA = 10000.0
QK_SCALE = QK_HEAD ** -0.5
LOG2E = 1.4426950408889634

LANES = 128
SUBLANES = 8
HEAD_PAD = 256
T_PAD = SUBLANES
VMEM_LIMIT = 56 * 1024 * 1024

ZC_Q, ZC_K, ZC_V, ZC_R, ZC_CQ, ZC_CKV, ZC_KRD, ZC_GA, ZC_END = 0, 256, 512, 1024, 1536, 1792, 1920, 2048, 2176
GC_Q, GC_K, GC_V, GC_R, GC_G, GC_END = 0, 256, 512, 1024, 1536, 1792

NT_DIMS = (((1,), (1,)), ((), ()))


def _layer_spec(arr, layer):
    shape = arr.shape[1:]
    nd = len(shape)
    return pl.BlockSpec((None,) + shape, lambda *_: (layer,) + (0,) * nd, pipeline_mode=pl.Buffered(1))


def _split_rows(tm, width, prompt_tiles):
    first = pl.BlockSpec((tm, width), lambda i: (jnp.minimum(i, prompt_tiles - 1), 0))
    second = pl.BlockSpec((tm, width), lambda i: (jnp.maximum(i - prompt_tiles, 0), 0))
    return first, second


def _pick_tile(n, candidates):
    for c in candidates:
        if n % c == 0:
            return c
    raise ValueError(f"no tile in {candidates} divides {n}")


def _rms(x, g):
    return x * lax.rsqrt(jnp.mean(x * x, axis=-1, keepdims=True) + EPS) * g


def _silu(x):
    return x * jax.nn.sigmoid(x)


def _split3(x):
    hi = x.astype(BF16)
    r1 = x - hi.astype(F32)
    mid = r1.astype(BF16)
    lo = (r1 - mid.astype(F32)).astype(BF16)
    return hi, mid, lo


def _fold_rows(x, op):
    while x.shape[0] > SUBLANES:
        h = x.shape[0] // 2
        x = op(x[:h], x[h:])
    return x


def _sum_rows(x):
    return jnp.sum(_fold_rows(x, jnp.add), axis=0, keepdims=True)


def _dot(a, b):
    return jnp.dot(a, b, preferred_element_type=F32)


def _dot_nt(a, b):
    return lax.dot_general(a, b, NT_DIMS, preferred_element_type=F32)


def _dot3(m, parts):
    return _dot(m, parts[0]) + _dot(m, parts[1]) + _dot(m, parts[2])


def _swiglu_into(acc_ref, xn, wgu_ref, wd_ref, ffn, ch):
    for c in range(ffn // ch):
        gate = _dot(xn, wgu_ref[:, c * ch:(c + 1) * ch])
        up = _dot(xn, wgu_ref[:, ffn + c * ch:ffn + (c + 1) * ch])
        hm = (_silu(gate) * up).astype(BF16)
        part = _dot(hm, wd_ref[c * ch:(c + 1) * ch, :])
        if c == 0:
            acc_ref[...] = part
        else:
            acc_ref[...] += part


def _ffn_kernel(x_ref, g_ref, wgu_ref, wd_ref, o_ref, acc_ref, *, ffn, ch):
    x = x_ref[...]
    _swiglu_into(acc_ref, _rms(x, g_ref[...]).astype(BF16), wgu_ref, wd_ref, ffn, ch)
    o_ref[...] = x + 0.5 * acc_ref[...]


def _ffn_split_kernel(xp_ref, xs_ref, g_ref, wgu_ref, wd_ref, o_ref, acc_ref, x_ref, *, ffn, ch, prompt_tiles):
    i = pl.program_id(0)

    @pl.when(i < prompt_tiles)
    def _():
        x_ref[...] = xp_ref[...]

    @pl.when(i >= prompt_tiles)
    def _():
        x_ref[...] = xs_ref[...]

    _ffn_kernel(x_ref, g_ref, wgu_ref, wd_ref, o_ref, acc_ref, ffn=ffn, ch=ch)


def _ffn_call(x, g, wgu, wd, layer):
    ffn = wd.shape[1]
    ch = _pick_tile(ffn, (256, 128))
    consts = [g, wgu, wd]
    if isinstance(x, tuple):
        xp, xs = x
        d = xp.shape[1]
        n = xp.shape[0] + xs.shape[0]
        tm = _pick_tile(math.gcd(xp.shape[0], xs.shape[0]), (512, 256, 128))
        pt = xp.shape[0] // tm
        body = functools.partial(_ffn_split_kernel, ffn=ffn, ch=ch, prompt_tiles=pt)
        x_specs, xs_args = list(_split_rows(tm, d, pt)), [xp, xs]
    else:
        n, d = x.shape
        tm = _pick_tile(n, (768, 512, 256, 128))
        body = functools.partial(_ffn_kernel, ffn=ffn, ch=ch)
        x_specs, xs_args = [pl.BlockSpec((tm, d), lambda i: (i, 0))], [x]
    return pl.pallas_call(
        body,
        grid=(n // tm,),
        in_specs=x_specs + [_layer_spec(c, layer) for c in consts],
        out_specs=pl.BlockSpec((tm, d), lambda i: (i, 0)),
        out_shape=jax.ShapeDtypeStruct((n, d), F32),
        scratch_shapes=[pltpu.VMEM((tm, d), F32)] * len(xs_args),
        compiler_params=pltpu.CompilerParams(
            dimension_semantics=("arbitrary",), vmem_limit_bytes=VMEM_LIMIT),
        name="ffn",
    )(*xs_args, *consts)


def _mixprep_kernel(h_ref, gmix_ref, win_ref, wgate_ref, bgate_ref, gcq_ref, wuq_ref,
                    gqn_ref, gqd_ref, gckv_ref, wuk_ref, gkn_ref, gkd_ref, t_ref,
                    gla_ref, ckvn_ref, kr_ref, qh_ref, kh_ref, vbt_ref, qabs_ref, *, parts):
    tm = h_ref.shape[0]
    sub = tm // parts
    for part in range(parts):
        rows = slice(part * sub, (part + 1) * sub)
        u = _rms(h_ref[rows, :], gmix_ref[...]).astype(BF16)
        z = _dot(u, win_ref[...])

        gla_ref[rows, GC_Q:GC_K] = z[:, ZC_Q:ZC_K] * (GLA_DK ** -0.5)
        gla_ref[rows, GC_K:GC_G] = z[:, ZC_K:ZC_CQ]
        gx = _dot(z[:, ZC_GA:ZC_END].astype(BF16), wgate_ref[...]) + bgate_ref[...]
        log_sig = jnp.minimum(gx, 0.0) - jnp.log(1.0 + jnp.exp(-jnp.abs(gx)))
        gla_ref[rows, GC_G:GC_END] = log_sig / GLA_TAU

        ckvn = _rms(z[:, ZC_CKV:ZC_KRD], gckv_ref[...])
        ckvn_ref[rows, :] = ckvn
        cb = ckvn.astype(BF16)
        vbt_ref[0, :, rows] = ckvn.T.astype(BF16)
        krd = z[:, ZC_KRD:ZC_GA]
        kr_ref[rows, :] = krd[:, 0:QK_ROPE]
        kr_ss = 0.5 * jnp.sum(krd * krd, axis=-1, keepdims=True)
        kn = _dot(cb, wuk_ref[...])

        cqn = _rms(z[:, ZC_CQ:ZC_CKV], gcq_ref[...]).astype(BF16)
        qq = _dot(cqn, wuq_ref[...])

        tab = t_ref[rows, :]
        gqn, gqd, gkn, gkd = gqn_ref[...], gqd_ref[...], gkn_ref[...], gkd_ref[...]
        low = lax.broadcasted_iota(jnp.int32, (1, LANES), 1) < QK_ROPE
        nh = MLA_HEADS * QK_NOPE
        for hh in range(MLA_HEADS):
            lo, hi = hh * QK_NOPE, (hh + 1) * QK_NOPE
            qn = qq[:, lo:hi]
            qd = qq[:, nh + lo:nh + hi]
            ss = jnp.sum(qn * qn, axis=-1, keepdims=True) + 0.5 * jnp.sum(qd * qd, axis=-1, keepdims=True)
            rq = lax.rsqrt(ss / QK_HEAD + EPS) * (QK_SCALE * LOG2E)
            qnope = qn * rq * gqn
            aq = qd * rq * gqd * tab
            qrope = jnp.where(low, aq + pltpu.roll(aq, QK_ROPE, axis=1), 0.0)
            qh_ref[rows, hh * HEAD_PAD:hh * HEAD_PAD + QK_NOPE] = qnope.astype(BF16)
            qh_ref[rows, hh * HEAD_PAD + QK_NOPE:(hh + 1) * HEAD_PAD] = qrope.astype(BF16)
            qabs = _dot_nt((qnope * gkn).astype(BF16), wuk_ref[:, lo:hi])
            qabs_ref[rows, lo:hi] = qabs.astype(BF16)

            knh = kn[:, lo:hi]
            rk = lax.rsqrt((jnp.sum(knh * knh, axis=-1, keepdims=True) + kr_ss) / QK_HEAD + EPS)
            kh_ref[rows, hh * HEAD_PAD:hh * HEAD_PAD + QK_NOPE] = (knh * rk * gkn).astype(BF16)
            ak = krd * rk * gkd * tab
            krope = jnp.where(low, ak + pltpu.roll(ak, QK_ROPE, axis=1), 0.0)
            kh_ref[rows, hh * HEAD_PAD + QK_NOPE:(hh + 1) * HEAD_PAD] = krope.astype(BF16)


def _mixprep_call(h, gmix, win, wgate, bgate, gcq, wuq, gqn, gqd, gckv, wuk, gkn, gkd, tab, tm, layer):
    n, d = h.shape
    row = lambda w: pl.BlockSpec((tm, w), lambda i: (i, 0))
    consts = [gmix, win, wgate, bgate, gcq, wuq, gqn, gqd, gckv, wuk, gkn, gkd]
    hq = MLA_HEADS * HEAD_PAD
    out_shape = [
        jax.ShapeDtypeStruct((n, GC_END), F32),
        jax.ShapeDtypeStruct((n, KV_LORA), F32),
        jax.ShapeDtypeStruct((n, QK_ROPE), F32),
        jax.ShapeDtypeStruct((n, hq), BF16),
        jax.ShapeDtypeStruct((n, hq), BF16),
        jax.ShapeDtypeStruct((n // tm, KV_LORA, tm), BF16),
        jax.ShapeDtypeStruct((n, MLA_HEADS * KV_LORA), BF16),
    ]
    out_specs = [row(s.shape[1]) for s in out_shape]
    out_specs[5] = pl.BlockSpec((1, KV_LORA, tm), lambda i: (i, 0, 0))
    return pl.pallas_call(
        functools.partial(_mixprep_kernel, parts=tm // LANES),
        grid=(n // tm,),
        in_specs=[row(d)] + [_layer_spec(c, layer) for c in consts] + [row(LANES)],
        out_specs=out_specs,
        out_shape=out_shape,
        compiler_params=pltpu.CompilerParams(
            dimension_semantics=("arbitrary",), vmem_limit_bytes=VMEM_LIMIT),
        name="mixprep",
    )(h, *consts, tab)


def _gla_out(o, ra, gout):
    outs = []
    for hh in range(GLA_HEADS):
        oh = o[:, hh * GLA_DV:(hh + 1) * GLA_DV]
        outs.append(_rms(oh, gout) * _silu(ra[:, hh * GLA_DV:(hh + 1) * GLA_DV]))
    return outs


def _gla_prompt_kernel(*refs, tc, nb):
    gin_refs, (gout_ref, o_ref, st_ref, s_ref) = refs[:nb], refs[nb:]
    i = pl.program_id(0)
    c_len = GLA_CHUNK
    kd = GLA_HEADS * GLA_DK
    vd = GLA_HEADS * GLA_DV

    @pl.when(i == 0)
    def _():
        s_ref[...] = jnp.zeros_like(s_ref)

    ri = lax.broadcasted_iota(jnp.int32, (c_len, c_len), 0)
    ci = lax.broadcasted_iota(jnp.int32, (c_len, c_len), 1)
    tri = (ci <= ri).astype(BF16)
    k_head = lax.broadcasted_iota(jnp.int32, (1, kd), 1) // GLA_DK
    v_head = lax.broadcasted_iota(jnp.int32, (1, vd), 1) // GLA_DV
    wide_r = lax.broadcasted_iota(jnp.int32, (c_len, kd), 0)
    wide_s = lax.broadcasted_iota(jnp.int32, (c_len, kd), 1) % c_len
    causal_wide = wide_s <= wide_r
    bd = (lax.broadcasted_iota(jnp.int32, (kd, vd), 0) // GLA_DK
          == lax.broadcasted_iota(jnp.int32, (kd, vd), 1) // GLA_DV)
    gout = gout_ref[...]

    for c, bb in ((c, bb) for c in range(tc // c_len) for bb in range(nb)):
        gin_ref = gin_refs[bb]
        rows = slice(c * c_len, (c + 1) * c_len)
        q = gin_ref[rows, GC_Q:GC_K]
        k = gin_ref[rows, GC_K:GC_V]
        v = gin_ref[rows, GC_V:GC_R]
        ra = gin_ref[rows, GC_R:GC_G]
        g = gin_ref[rows, GC_G:GC_END]
        b = _dot3(tri, _split3(g))
        b_last = b[c_len - 1:c_len, :]
        q_t = (q * jnp.exp(b)).astype(BF16)
        k_t = k * jnp.exp(-b)
        k_dec = k * jnp.exp(b_last - b)
        vb = v.astype(BF16)

        kstack = jnp.concatenate(
            [jnp.where(k_head == hh, k_t, 0.0) for hh in range(GLA_HEADS)], axis=0).astype(BF16)
        a = jnp.where(causal_wide, _dot_nt(q_t, kstack), 0.0)
        vbd = jnp.concatenate(
            [jnp.where(v_head == hh, v, 0.0) for hh in range(GLA_HEADS)], axis=0).astype(BF16)
        s_prev = s_ref[bb]
        o = _dot(a.astype(BF16), vbd) + _dot(q_t, s_prev.astype(BF16))

        kpad = jnp.concatenate(
            [k_dec, jnp.broadcast_to(b_last, (SUBLANES, kd)),
             jnp.zeros((LANES - c_len - SUBLANES, kd), F32)], axis=0)
        kdt = kpad.T
        decay_col = jnp.exp(kdt[:, c_len:c_len + 1])
        vpad = jnp.concatenate([vb, jnp.zeros((LANES - c_len, vd), BF16)], axis=0)
        ds = _dot(kdt.astype(BF16), vpad)
        s_ref[bb] = decay_col * s_prev + jnp.where(bd, ds, 0.0)

        outs = _gla_out(o, ra, gout)
        for hh in range(GLA_HEADS):
            o_ref[bb, rows, hh * GLA_DV:(hh + 1) * GLA_DV] = outs[hh].astype(BF16)

    @pl.when(i == pl.num_programs(0) - 1)
    def _():
        for bb in range(nb):
            for hh in range(GLA_HEADS):
                st_ref[bb, hh] = s_ref[bb, hh * GLA_DK:(hh + 1) * GLA_DK, hh * GLA_DV:(hh + 1) * GLA_DV]


def _gla_prompt_call(gla, gout, batch, seq, layer):
    assert batch <= SUBLANES
    tc = _pick_tile(seq, (256, 128, 64))
    nc = seq // tc
    vd = GLA_HEADS * GLA_DV
    seq_spec = lambda bb: pl.BlockSpec((tc, GC_END), lambda i: (bb * nc + i, 0))
    o, st = pl.pallas_call(
        functools.partial(_gla_prompt_kernel, tc=tc, nb=batch),
        grid=(nc,),
        in_specs=[seq_spec(bb) for bb in range(batch)] + [_layer_spec(gout, layer)],
        out_specs=[
            pl.BlockSpec((batch, tc, vd), lambda i: (0, i, 0)),
            pl.BlockSpec((batch, GLA_HEADS, GLA_DK, GLA_DV), lambda i: (0, 0, 0, 0)),
        ],
        out_shape=[
            jax.ShapeDtypeStruct((batch, seq, vd), BF16),
            jax.ShapeDtypeStruct((batch, GLA_HEADS, GLA_DK, GLA_DV), F32),
        ],
        scratch_shapes=[pltpu.VMEM((batch, GLA_HEADS * GLA_DK, vd), F32)],
        compiler_params=pltpu.CompilerParams(
            dimension_semantics=("arbitrary",), vmem_limit_bytes=VMEM_LIMIT),
        name="gla_prompt",
    )(*([gla] * batch), gout)
    return o.reshape(batch * seq, vd), st


def _gla_sample_kernel(gin_ref, s0_ref, gout_ref, o_ref, sn_ref, *, nreq):
    rows = nreq * T_PAD
    kd = GLA_HEADS * GLA_DK
    q = gin_ref[:, GC_Q:GC_K]
    k = gin_ref[:, GC_K:GC_V]
    v = gin_ref[:, GC_V:GC_R]
    ra = gin_ref[:, GC_R:GC_G]
    g = gin_ref[:, GC_G:GC_END]
    ri = lax.broadcasted_iota(jnp.int32, (rows, rows), 0)
    ci = lax.broadcasted_iota(jnp.int32, (rows, rows), 1)
    same = (ri // T_PAD) == (ci // T_PAD)
    causal = same & (ci <= ri)
    g3 = _split3(g)
    b = _dot3(causal.astype(BF16), g3)
    b_last = _dot3(same.astype(BF16), g3)
    q_t = q * jnp.exp(b)
    k_t = (k * jnp.exp(-b)).astype(BF16)
    k_dec = k * jnp.exp(b_last - b)
    vb = v.astype(BF16)
    k_head = lax.broadcasted_iota(jnp.int32, (1, kd), 1) // GLA_DK

    o_intra = []
    for hh in range(GLA_HEADS):
        a = _dot_nt(jnp.where(k_head == hh, q_t, 0.0).astype(BF16), k_t)
        a = jnp.where(causal, a, 0.0).astype(BF16)
        o_intra.append(_dot(a, vb[:, hh * GLA_DV:(hh + 1) * GLA_DV]))

    o_inter = []
    for r in range(nreq):
        qr = q_t[r * T_PAD:(r + 1) * T_PAD, :]
        qm = jnp.concatenate(
            [jnp.where(k_head == hh, qr, 0.0) for hh in range(GLA_HEADS)], axis=0).astype(BF16)
        o_inter.append(_dot(qm, s0_ref[r * kd:(r + 1) * kd, :].astype(BF16)))

    gout = gout_ref[...]
    for hh in range(GLA_HEADS):
        inter = jnp.concatenate(
            [o_inter[r][hh * T_PAD:(hh + 1) * T_PAD, :] for r in range(nreq)], axis=0)
        oh = o_intra[hh] + inter
        y = _rms(oh, gout) * _silu(ra[:, hh * GLA_DV:(hh + 1) * GLA_DV])
        o_ref[:, hh * GLA_DV:(hh + 1) * GLA_DV] = y.astype(BF16)

    kdt = k_dec.T
    blt = b_last.T
    lane = lax.broadcasted_iota(jnp.int32, (1, rows), 1)
    for hh in range(GLA_HEADS):
        kh = kdt[hh * GLA_DK:(hh + 1) * GLA_DK, :]
        bh = blt[hh * GLA_DK:(hh + 1) * GLA_DK, :]
        lhs = jnp.concatenate(
            [jnp.where(lane // T_PAD == r, kh, 0.0) for r in range(nreq)], axis=0).astype(BF16)
        ds = _dot(lhs, vb[:, hh * GLA_DV:(hh + 1) * GLA_DV])
        for r in range(nreq):
            dec = jnp.exp(jnp.sum(jnp.where(lane == r * T_PAD, bh, 0.0), axis=-1, keepdims=True))
            lo = r * kd + hh * GLA_DK
            sn_ref[lo:lo + GLA_DK, :] = dec * s0_ref[lo:lo + GLA_DK, :] + ds[r * GLA_DK:(r + 1) * GLA_DK, :]


def _gla_sample_call(gla_pad, s0, gout, dec_batch, layer):
    nreq = _pick_tile(dec_batch, (16, 8, 4, 2, 1))
    rows = nreq * T_PAD
    kd = GLA_HEADS * GLA_DK
    vd = GLA_HEADS * GLA_DV
    return pl.pallas_call(
        functools.partial(_gla_sample_kernel, nreq=nreq),
        grid=(dec_batch // nreq,),
        in_specs=[
            pl.BlockSpec((rows, GC_END), lambda i: (i, 0)),
            pl.BlockSpec((None, nreq * kd, GLA_DV), lambda i: (layer, i, 0)),
            _layer_spec(gout, layer),
        ],
        out_specs=[
            pl.BlockSpec((rows, vd), lambda i: (i, 0)),
            pl.BlockSpec((nreq * kd, GLA_DV), lambda i: (i, 0)),
        ],
        out_shape=[
            jax.ShapeDtypeStruct((dec_batch * T_PAD, vd), BF16),
            jax.ShapeDtypeStruct((dec_batch * kd, GLA_DV), F32),
        ],
        compiler_params=pltpu.CompilerParams(
            dimension_semantics=("arbitrary",), vmem_limit_bytes=VMEM_LIMIT),
        name="gla_sample",
    )(gla_pad, s0, gout)


def _mla_prompt_kernel(q_ref, k_ref, vt_ref, wuvt_ref, o_ref, qt_ref, m_ref, l_ref, acc_ref, *, tq):
    i = pl.program_id(1)
    qt_ref[...] = q_ref[...].astype(F32).T.astype(BF16)
    m_ref[...] = jnp.full(m_ref.shape, -jnp.inf, F32)
    l_ref[...] = jnp.zeros_like(l_ref)
    acc_ref[...] = jnp.zeros_like(acc_ref)
    key_i = lax.broadcasted_iota(jnp.int32, (tq, tq), 0)
    qry_i = lax.broadcasted_iota(jnp.int32, (tq, tq), 1)
    diag_ok = key_i <= qry_i
    qw = min(tq, 2 * LANES)

    def block(j, masked):
        k0 = pl.multiple_of(j * tq, tq)
        kblk = k_ref[pl.ds(k0, tq), :]
        ps, alphas = [], []
        for hh, part in ((hh, part) for hh in range(MLA_HEADS) for part in range(tq // qw)):
            qs = slice(part * qw, (part + 1) * qw)
            st = _dot(kblk[:, hh * HEAD_PAD:(hh + 1) * HEAD_PAD], qt_ref[hh * HEAD_PAD:(hh + 1) * HEAD_PAD, qs])
            if masked:
                st = jnp.where(diag_ok[:, qs], st, -jnp.inf)
            m_prev = m_ref[hh:hh + 1, qs]
            m_new = jnp.maximum(m_prev, jnp.max(st, axis=0, keepdims=True))
            alpha = jnp.exp2(m_prev - m_new)
            p = jnp.exp2(st - m_new)
            l_ref[hh:hh + 1, qs] = alpha * l_ref[hh:hh + 1, qs] + jnp.sum(p, axis=0, keepdims=True)
            m_ref[hh:hh + 1, qs] = m_new
            ps.append(p.astype(BF16))
            alphas.append(alpha)
        pv = _dot(vt_ref[j], jnp.concatenate(ps, axis=1))
        acc_ref[...] = acc_ref[...] * jnp.concatenate(alphas, axis=1) + pv

    def body(j, carry):
        block(j, False)
        return carry

    lax.fori_loop(0, i, body, 0)
    block(i, True)
    inv_l = 1.0 / jnp.concatenate([l_ref[hh:hh + 1, :] for hh in range(MLA_HEADS)], axis=1)
    lat_t = (acc_ref[...] * inv_l).astype(BF16)
    out_t = jnp.concatenate(
        [_dot(wuvt_ref[hh * V_HEAD:(hh + 1) * V_HEAD, :], lat_t[:, hh * tq:(hh + 1) * tq])
         for hh in range(MLA_HEADS)], axis=0)
    o_ref[...] = out_t.T.astype(BF16)


def _mla_prompt_call(qh, kh, vbt, wuvt, batch, seq, tq, layer):
    nq = seq // tq
    hq = MLA_HEADS * HEAD_PAD
    od = MLA_HEADS * V_HEAD
    return pl.pallas_call(
        functools.partial(_mla_prompt_kernel, tq=tq),
        grid=(batch, nq),
        in_specs=[
            pl.BlockSpec((tq, hq), lambda b, i: (b * nq + i, 0)),
            pl.BlockSpec((seq, hq), lambda b, i: (b, 0)),
            pl.BlockSpec((nq, KV_LORA, tq), lambda b, i: (b, 0, 0)),
            _layer_spec(wuvt, layer),
        ],
        out_specs=pl.BlockSpec((tq, od), lambda b, i: (b * nq + i, 0)),
        out_shape=jax.ShapeDtypeStruct((batch * seq, od), BF16),
        scratch_shapes=[
            pltpu.VMEM((hq, tq), BF16),
            pltpu.VMEM((SUBLANES, tq), F32),
            pltpu.VMEM((SUBLANES, tq), F32),
            pltpu.VMEM((KV_LORA, MLA_HEADS * tq), F32),
        ],
        compiler_params=pltpu.CompilerParams(
            dimension_semantics=("arbitrary", "arbitrary"), vmem_limit_bytes=VMEM_LIMIT),
        name="mla_prompt",
    )(qh, kh, vbt, wuvt)


def _mla_sample_kernel(pt_ref, qabs_ref, qpe_ref, cnew_ref, krnew_ref, wukt_ref, gpe_ref, cos_ref, sin_ref,
                       wuv_ref, ckv_hbm, krt_hbm, o_ref, cbuf, kbuf, cbf, sc_ref, rt_ref, sem,
                       *, layer, n_pages, page, kc, tw):
    r = pl.program_id(0)
    nreq = pl.num_programs(0)
    slot = r % 2
    past = n_pages * page
    tail = LANES
    ppc = kc // page
    n_chunks = past // kc

    def page_copies(req, sl, c, i):
        p = c * ppc + i
        pg = pt_ref[req * n_pages + p]
        dst = pl.ds(p * page, page) if isinstance(p, int) else pl.ds(pl.multiple_of(p * page, page), page)
        return (pltpu.make_async_copy(ckv_hbm.at[layer, pg], cbuf.at[sl, dst], sem.at[sl, 0, c]),
                pltpu.make_async_copy(krt_hbm.at[layer, pg], kbuf.at[sl, p], sem.at[sl, 1, c]))

    half = QK_ROPE // 2
    nxt = jnp.minimum(r + 1, nreq - 1)

    @pl.when(r == 0)
    def _():
        for sl in range(2):
            cbuf[sl, past:past + tail, :] = jnp.zeros((tail, KV_LORA), F32)

        def first(p, carry):
            for c in range(n_chunks):
                for cp in page_copies(0, 0, c, p):
                    cp.start()
            return carry
        lax.fori_loop(0, ppc, first, 0)
        g1, g2 = gpe_ref[0:half, :], gpe_ref[half:QK_ROPE, :]
        rt_ref[0] = cos_ref[...] * g1
        rt_ref[1] = sin_ref[...] * g2
        rt_ref[2] = sin_ref[...] * g1
        rt_ref[3] = cos_ref[...] * g2

    cbuf[slot, past:past + T_PAD, :] = cnew_ref[0]
    kbuf[slot, n_pages] = krnew_ref[0]

    nrow = MLA_HEADS * T_PAD
    lhs = jnp.concatenate([wukt_ref[...], qabs_ref[0]], axis=0)
    qpe = qpe_ref[0]
    nk = MLA_HEADS * QK_NOPE

    def features(k0, width):
        cb = cbuf[slot, k0:k0 + width, :].astype(BF16)
        cbf[k0:k0 + width, :] = cb
        return _dot_nt(lhs, cb)

    def scores(res_rows, krt, k0, width, is_tail):
        ss_kr = _sum_rows(krt * krt)
        x1, x2 = krt[0:half], krt[half:QK_ROPE]
        cols = slice(k0, k0 + width)
        rot = jnp.concatenate([x1 * rt_ref[0, :, cols] - x2 * rt_ref[1, :, cols],
                               x1 * rt_ref[2, :, cols] + x2 * rt_ref[3, :, cols]], axis=0).astype(BF16)
        s = res_rows(nk, nk + nrow) + _dot(qpe, rot)
        rks = []
        for hh in range(MLA_HEADS):
            knh = res_rows(hh * QK_NOPE, (hh + 1) * QK_NOPE)
            rk = lax.rsqrt((_sum_rows(knh * knh) + ss_kr) / QK_HEAD + EPS)
            rks.append(jnp.broadcast_to(rk, (T_PAD, width)))
        s = s * jnp.concatenate(rks, axis=0)
        if is_tail:
            tok = lax.broadcasted_iota(jnp.int32, (nrow, width), 0) % T_PAD
            key = lax.broadcasted_iota(jnp.int32, (nrow, width), 1)
            s = jnp.where(key <= tok, s, -jnp.inf)
        sc_ref[:, cols] = s

    ppt = tw // page
    for c in range(n_chunks):
        for i in range(ppc):
            for cp in page_copies(nxt, 1 - slot, c, i):
                cp.start()
        for i in range(ppc):
            for cp in page_copies(r, slot, c, i):
                cp.wait()
        for t in range(kc // tw):
            k0 = c * kc + t * tw
            res = features(k0, tw)
            krt = jnp.concatenate([kbuf[slot, k0 // page + i] for i in range(ppt)], axis=1)
            scores(lambda lo, hi: res[lo:hi], krt, k0, tw, False)
    res_tail = features(past, tail)
    scores(lambda lo, hi: res_tail[lo:hi], kbuf[slot, n_pages], past, tail, True)

    s_all = sc_ref[...]
    m = jnp.max(s_all, axis=-1, keepdims=True)
    p = jnp.exp2(s_all - m)
    l = jnp.sum(p, axis=-1, keepdims=True)
    lat = _dot(p.astype(BF16), cbf[...]) / l
    for hh in range(MLA_HEADS):
        lh = lat[hh * T_PAD:(hh + 1) * T_PAD, :].astype(BF16)
        o_ref[0, hh * T_PAD:(hh + 1) * T_PAD, :] = _dot(lh, wuv_ref[:, hh * V_HEAD:(hh + 1) * V_HEAD]).astype(BF16)

    @pl.when(r == nreq - 1)
    def _():
        def drain(i, carry):
            for c in range(n_chunks):
                for cp in page_copies(nxt, 1 - slot, c, i):
                    cp.wait()
            return carry
        lax.fori_loop(0, ppc, drain, 0)


def _mla_sample_call(page_table, qabs, qpe, cnew, krnew_t, wukt, gpe, cos_t, sin_t, wuv, cache_ckv, cache_krope_t,
                     layer):
    dec_batch, n_pages = page_table.shape
    page = cache_ckv.shape[2]
    assert page == LANES and cache_krope_t.shape[2:] == (QK_ROPE, page)
    past = n_pages * page
    kc = _pick_tile(past // 2, (4096, 2048, 1024, 512, 256, 128))
    nrow = MLA_HEADS * T_PAD
    total = past + LANES
    req = lambda w: pl.BlockSpec((1, nrow, w), lambda r, pt: (r, 0, 0))
    const = lambda shape: pl.BlockSpec(shape, lambda r, pt: (0,) * len(shape), pipeline_mode=pl.Buffered(1))
    grid_spec = pltpu.PrefetchScalarGridSpec(
        num_scalar_prefetch=1,
        grid=(dec_batch,),
        in_specs=[
            req(KV_LORA), req(QK_ROPE),
            pl.BlockSpec((1, T_PAD, KV_LORA), lambda r, pt: (r, 0, 0)),
            pl.BlockSpec((1, QK_ROPE, LANES), lambda r, pt: (r, 0, 0)),
            _layer_spec(wukt, layer), _layer_spec(gpe, layer), const(cos_t.shape), const(sin_t.shape),
            _layer_spec(wuv, layer),
            pl.BlockSpec(memory_space=pl.ANY), pl.BlockSpec(memory_space=pl.ANY),
        ],
        out_specs=pl.BlockSpec((1, nrow, V_HEAD), lambda r, pt: (r, 0, 0)),
        scratch_shapes=[
            pltpu.VMEM((2, total, KV_LORA), F32),
            pltpu.VMEM((2, n_pages + 1, QK_ROPE, page), F32),
            pltpu.VMEM((total, KV_LORA), BF16),
            pltpu.VMEM((nrow, total), F32),
            pltpu.VMEM((4, QK_ROPE // 2, total), F32),
            pltpu.SemaphoreType.DMA((2, 2, past // kc)),
        ],
    )
    return pl.pallas_call(
        functools.partial(_mla_sample_kernel, layer=layer, n_pages=n_pages, page=page, kc=kc, tw=kc),
        grid_spec=grid_spec,
        out_shape=jax.ShapeDtypeStruct((dec_batch, nrow, V_HEAD), BF16),
        compiler_params=pltpu.CompilerParams(
            dimension_semantics=("arbitrary",), vmem_limit_bytes=VMEM_LIMIT),
        name="mla_sample",
    )(page_table.reshape(-1), qabs, qpe, cnew, krnew_t, wukt, gpe, cos_t, sin_t, wuv, cache_ckv, cache_krope_t)


def _mid_kernel(h_ref, oap_ref, obp_ref, pp_ref, oas_ref, obs_ref, ps_ref, wa_ref, wb_ref, g2_ref, wgu_ref, wd_ref,
                gp_ref, wpg_ref, wp_ref, *rest, ffn, ch, prompt_tiles, split_out):
    outs, (x_ref, acc_ref, pe_ref) = rest[:-3], rest[-3:]
    i = pl.program_id(0)

    def load(oa_ref, ob_ref, p_ref):
        x_ref[...] = h_ref[...] + _dot(oa_ref[...], wa_ref[...]) + _dot(ob_ref[...], wb_ref[...])
        pe_ref[...] = p_ref[...].astype(BF16)

    @pl.when(i < prompt_tiles)
    def _():
        load(oap_ref, obp_ref, pp_ref)

    @pl.when(i >= prompt_tiles)
    def _():
        load(oas_ref, obs_ref, ps_ref)

    x = x_ref[...]
    _swiglu_into(acc_ref, _rms(x, g2_ref[...]).astype(BF16), wgu_ref, wd_ref, ffn, ch)
    x = x + 0.5 * acc_ref[...]
    gate = jax.nn.sigmoid(_dot(_rms(x, gp_ref[...]).astype(BF16), wpg_ref[...]))
    y = x + gate * _dot(pe_ref[...], wp_ref[...])
    if split_out:
        @pl.when(i < prompt_tiles)
        def _():
            outs[0][...] = y

        @pl.when(i >= prompt_tiles)
        def _():
            outs[1][...] = y
    else:
        outs[0][...] = y


def _mid_call(h, oa_p, ob_p, p_p, oa_s, ob_s, p_s, wa, wb, g2, wgu, wd, gp, wpg, wp, layer, split_out):
    n, d = h.shape
    ffn = wd.shape[1]
    np_rows, ns_rows = oa_p.shape[0], oa_s.shape[0]
    tm = _pick_tile(math.gcd(np_rows, ns_rows), (512, 256, 128))
    assert np_rows + ns_rows == n
    pt = np_rows // tm
    ch = _pick_tile(ffn, (256, 128))
    row = pl.BlockSpec((tm, d), lambda i: (i, 0))
    oa_specs = _split_rows(tm, oa_p.shape[1], pt)
    ob_specs = _split_rows(tm, ob_p.shape[1], pt)
    pw = p_p.shape[2]
    pp_spec = pl.BlockSpec((None, tm, pw), lambda i: (layer, jnp.minimum(i, pt - 1), 0))
    ps_spec = pl.BlockSpec((None, tm, pw), lambda i: (layer, jnp.maximum(i - pt, 0), 0))
    consts = [wa, wb, g2, wgu, wd, gp, wpg, wp]
    if split_out:
        out_specs = list(_split_rows(tm, d, pt))
        out_shape = [jax.ShapeDtypeStruct((np_rows, d), F32), jax.ShapeDtypeStruct((ns_rows, d), F32)]
    else:
        out_specs, out_shape = row, jax.ShapeDtypeStruct((n, d), F32)
    return pl.pallas_call(
        functools.partial(_mid_kernel, ffn=ffn, ch=ch, prompt_tiles=pt, split_out=split_out),
        grid=(n // tm,),
        in_specs=([row, oa_specs[0], ob_specs[0], pp_spec, oa_specs[1], ob_specs[1], ps_spec]
                  + [_layer_spec(c, layer) for c in consts]),
        out_specs=out_specs,
        out_shape=out_shape,
        scratch_shapes=[pltpu.VMEM((tm, d), F32), pltpu.VMEM((tm, d), F32), pltpu.VMEM((tm, pw), BF16)],
        compiler_params=pltpu.CompilerParams(
            dimension_semantics=("arbitrary",), vmem_limit_bytes=VMEM_LIMIT),
        name="mid",
    )(h, oa_p, ob_p, p_p, oa_s, ob_s, p_s, *consts)


def _rope_tables(pos):
    inv = ROPE_THETA ** (-jnp.arange(0, QK_ROPE, 2, dtype=F32) / QK_ROPE)
    ang = pos.astype(F32)[:, None] * inv[None, :]
    return jnp.cos(ang), jnp.sin(ang)


def _dup_rope(x):
    x1, x2 = x[..., :QK_ROPE // 2], x[..., QK_ROPE // 2:]
    return jnp.concatenate([x1, x2, x2, x1], axis=-1)


def kernel(x_prompt, x_sample, cache_ckv, cache_krope, state_gla, page_table, p_prompt, p_sample, g_ffn1, w_ffn1_gu, w_ffn1_d, g_mix, w_in, w_gla_gate, b_gla_gate, g_gla_out, g_cq, w_uq, g_ckv, w_uk, w_uv, g_q, g_k, w_out, g_ffn2, w_ffn2_gu, w_ffn2_d, g_ple, w_ple_gate, w_ple):
    batch, seq, d = x_prompt.shape
    dec_batch, dec_seq, _ = x_sample.shape
    depth = w_in.shape[0]
    n_pages, page = page_table.shape[1], cache_ckv.shape[2]
    past = n_pages * page
    np_rows = batch * seq
    ns_rows = dec_batch * dec_seq
    assert dec_seq <= T_PAD

    kd = GLA_HEADS * GLA_DK
    vd = GLA_HEADS * GLA_DV
    o = 0
    cols = {}
    for name, w in (("q", kd), ("k", kd), ("v", vd), ("r", vd), ("ga", GLA_GATE_RANK), ("cq", Q_LORA),
                    ("ckv", KV_LORA), ("kr", QK_ROPE)):
        cols[name] = w_in[:, :, o:o + w]
        o += w
    win_p = jnp.concatenate(
        [cols["q"], cols["k"], cols["v"], cols["r"], cols["cq"], cols["ckv"], _dup_rope(cols["kr"]), cols["ga"],
         jnp.zeros((depth, d, LANES - GLA_GATE_RANK), F32)], axis=-1).astype(BF16)
    wgate_p = jnp.concatenate(
        [w_gla_gate, jnp.zeros((depth, LANES - GLA_GATE_RANK, kd), F32)], axis=1).astype(BF16)
    wuq4 = w_uq.reshape(depth, Q_LORA, MLA_HEADS, QK_HEAD)
    wuq_p = jnp.concatenate(
        [wuq4[..., :QK_NOPE].reshape(depth, Q_LORA, -1), _dup_rope(wuq4[..., QK_NOPE:]).reshape(depth, Q_LORA, -1)],
        axis=-1).astype(BF16)
    wuk_b = w_uk.astype(BF16)
    wukt_b = jnp.swapaxes(w_uk, 1, 2).astype(BF16)
    wuv_b = w_uv.astype(BF16)
    wuvt_b = jnp.swapaxes(w_uv, 1, 2).astype(BF16)
    wout_a, wout_b = w_out[:, :vd].astype(BF16), w_out[:, vd:].astype(BF16)
    wf1gu, wf1d = w_ffn1_gu.astype(BF16), w_ffn1_d.astype(BF16)
    wf2gu, wf2d = w_ffn2_gu.astype(BF16), w_ffn2_d.astype(BF16)
    wpg_b, wp_b = w_ple_gate.astype(BF16), w_ple.astype(BF16)
    row2 = lambda a: a.reshape(depth, 1, -1)
    gqn, gqd = row2(g_q[:, :QK_NOPE]), row2(_dup_rope(g_q[:, QK_NOPE:]))
    gkn, gkd = row2(g_k[:, :QK_NOPE]), row2(_dup_rope(g_k[:, QK_NOPE:]))
    gpe_col = g_k[:, QK_NOPE:].reshape(depth, QK_ROPE, 1)
    gf1, gmix, bgate, gcq, gckv, gout, gf2, gple = (
        row2(a) for a in (g_ffn1, g_mix, b_gla_gate, g_cq, g_ckv, g_gla_out, g_ffn2, g_ple))

    def rope_rows(pos, reps):
        cs, sn = _rope_tables(pos)
        return jnp.tile(jnp.concatenate([cs, cs, -sn, sn], axis=-1), (reps, 1))

    tab = jnp.concatenate([rope_rows(jnp.arange(seq), batch), rope_rows(past + jnp.arange(dec_seq), dec_batch)])
    cs_k, sn_k = _rope_tables(jnp.arange(past + LANES))
    cos_t, sin_t = cs_k.T, sn_k.T

    x = (x_prompt.reshape(np_rows, d), x_sample.reshape(ns_rows, d))
    pp_all = p_prompt.reshape(depth, np_rows, -1)
    ps_all = p_sample.reshape(depth, ns_rows, -1)
    s0_all = state_gla.reshape(depth, dec_batch * kd, GLA_DV)

    pad_t = lambda a: jnp.pad(a.reshape(dec_batch, dec_seq, -1), ((0, 0), (0, T_PAD - dec_seq), (0, 0)))
    unpad = lambda a: a.reshape(dec_batch, T_PAD, -1)[:, :dec_seq].reshape(ns_rows, -1)

    def heads_rows(a, width):
        a = a.reshape(dec_batch, dec_seq, MLA_HEADS, width).transpose(0, 2, 1, 3)
        a = jnp.pad(a, ((0, 0), (0, 0), (0, T_PAD - dec_seq), (0, 0)))
        return a.reshape(dec_batch, MLA_HEADS * T_PAD, width)

    cache_krope_t = jnp.swapaxes(cache_krope, 2, 3)
    tile = _pick_tile(math.gcd(seq, ns_rows), (512, 256, 128))

    outs = [[] for _ in range(6)]
    for l in range(depth):
        h1 = _ffn_call(x, gf1, wf1gu, wf1d, l)
        gla, ckvn, kr, qh, kh, vbt, qabs = _mixprep_call(
            h1, gmix, win_p, wgate_p, bgate, gcq, wuq_p, gqn, gqd, gckv, wuk_b, gkn, gkd, tab, tile, l)

        oa_p, st_p = _gla_prompt_call(gla, gout, batch, seq, l)
        oa_s, st_s = _gla_sample_call(pad_t(gla[np_rows:]).reshape(dec_batch * T_PAD, -1), s0_all, gout, dec_batch, l)
        ob_p = _mla_prompt_call(qh, kh, vbt, wuvt_b, batch, seq, tile, l)
        qh_s = qh[np_rows:].reshape(ns_rows, MLA_HEADS, HEAD_PAD)[:, :, QK_NOPE:QK_NOPE + QK_ROPE]
        krnew_t = jnp.pad(jnp.swapaxes(kr[np_rows:].reshape(dec_batch, dec_seq, QK_ROPE), 1, 2),
                          ((0, 0), (0, 0), (0, LANES - dec_seq)))
        ob_s = _mla_sample_call(
            page_table, heads_rows(qabs[np_rows:], KV_LORA), heads_rows(qh_s, QK_ROPE), pad_t(ckvn[np_rows:]),
            krnew_t, wukt_b, gpe_col, cos_t, sin_t, wuv_b, cache_ckv, cache_krope_t, l)
        ob_s = ob_s.reshape(dec_batch, MLA_HEADS, T_PAD, V_HEAD)[:, :, :dec_seq].transpose(0, 2, 1, 3)

        x = _mid_call(h1, oa_p, ob_p, pp_all, unpad(oa_s), ob_s.reshape(ns_rows, -1), ps_all, wout_a, wout_b,
                      gf2, wf2gu, wf2d, gple, wpg_b, wp_b, l, split_out=(l == depth - 1))

        outs[0].append(ckvn[:np_rows].reshape(batch, seq, KV_LORA))
        outs[1].append(kr[:np_rows].reshape(batch, seq, QK_ROPE))
        outs[2].append(st_p)
        outs[3].append(ckvn[np_rows:].reshape(dec_batch, dec_seq, KV_LORA))
        outs[4].append(kr[np_rows:].reshape(dec_batch, dec_seq, QK_ROPE))
        outs[5].append(st_s.reshape(dec_batch, GLA_HEADS, GLA_DK, GLA_DV))

    y_prompt, y_sample = x
    return (y_prompt.reshape(batch, seq, d), y_sample.reshape(dec_batch, dec_seq, d),
            *(jnp.stack(o) for o in outs))
```
